```python
import jax
import jax.numpy as jnp
from jax import lax
import numpy as np

D_MODEL = 4096
BATCH = 1
SEQ = 8192
DEPTH = 2

GRID_W = 64
CTX_LEN = 256
ROPE_THETA = 10000.0
Q_BLOCK = 128

GQA_HEADS = 8
GQA_KV_HEADS = 2
GQA_HEAD_DIM = 128

MLA_HEADS = 8
MLA_Q_RANK = 896
MLA_KV_RANK = 512
MLA_NOPE_DIM = 128
MLA_ROPE_DIM = 64
MLA_V_DIM = 128
MLA_QK_DIM = MLA_NOPE_DIM + MLA_ROPE_DIM

RWKV_HEADS = 16
RWKV_HEAD_DIM = 64
RWKV_WIDTH = RWKV_HEADS * RWKV_HEAD_DIM
RWKV_DECAY_RANK = 128
RWKV_ICLR_RANK = 128
RWKV_GATE_RANK = 480

BRANCH_WIDTH = 1024
N_BRANCHES = 3

N_EXPERTS = 32
TOP_K = 4
EXPERT_FF = 512
SWIGLU_ALPHA = 1.702
SWIGLU_LIMIT = 7.0
DISPATCH_BLOCK = 128

DEEPNORM_ALPHA = (2 * DEPTH) ** 0.25
DEEPNORM_BETA = (8 * DEPTH) ** -0.25
LN_EPS = 1e-5
RMS_EPS = 1e-6
GN_EPS = 64e-5

GQA_SPLITS = (GQA_HEADS * GQA_HEAD_DIM, GQA_KV_HEADS * GQA_HEAD_DIM, GQA_KV_HEADS * GQA_HEAD_DIM)
MLA_SPLITS = (MLA_Q_RANK, MLA_KV_RANK, MLA_ROPE_DIM)
RWKV_SPLITS = (RWKV_WIDTH, RWKV_WIDTH, RWKV_WIDTH, RWKV_DECAY_RANK, RWKV_DECAY_RANK,
               RWKV_ICLR_RANK, RWKV_ICLR_RANK, RWKV_GATE_RANK)
RWKV_IN = sum(RWKV_SPLITS)
GATE_SPLITS = (D_MODEL,) * N_BRANCHES
IN_SPLITS = GQA_SPLITS + MLA_SPLITS + (RWKV_IN, N_BRANCHES * D_MODEL)
IN_WIDTH = sum(IN_SPLITS)

kernel_name = "hybrid_gqa_mla_rwkv7_moe_diffusion_block"


def _split(x, sizes):
    return jnp.split(x, np.cumsum(sizes)[:-1].tolist(), axis=-1)


def _layernorm(x, g, b):
    xf = x.astype(jnp.float32)
    mu = jnp.mean(xf, -1, keepdims=True)
    var = jnp.mean(jnp.square(xf - mu), -1, keepdims=True)
    return ((xf - mu) * lax.rsqrt(var + LN_EPS) * g + b).astype(x.dtype)


def _rmsnorm(x, g):
    xf = x.astype(jnp.float32)
    return (xf * lax.rsqrt(jnp.mean(jnp.square(xf), -1, keepdims=True) + RMS_EPS) * g).astype(x.dtype)


def _axial_rope_tables(n_tokens, rot_dim):
    n_freq = rot_dim // 4
    n_rows = n_tokens // GRID_W
    row = jnp.repeat(jnp.arange(n_rows, dtype=jnp.float32), GRID_W)
    col = (jnp.arange(n_tokens) % GRID_W).astype(jnp.float32)
    freqs = ROPE_THETA ** (-jnp.arange(n_freq, dtype=jnp.float32) / n_freq)
    ang = jnp.stack([row[:, None] * freqs, col[:, None] * freqs], axis=1)
    return jnp.cos(ang), jnp.sin(ang)


def _apply_axial_rope(x, rope):
    cos, sin = rope
    b, t, h, dim = x.shape
    n_freq = dim // 4
    xf = x.astype(jnp.float32).reshape(b, t, h, 2, 2, n_freq)
    x1, x2 = xf[..., 0, :], xf[..., 1, :]
    cs, sn = cos[None, :, None], sin[None, :, None]
    out = jnp.stack([x1 * cs - x2 * sn, x2 * cs + x1 * sn], axis=-2)
    return out.reshape(b, t, h, dim).astype(x.dtype)


def _blocked_attention(q, k, v, scale):
    b, tq, hk, grp, d = q.shape
    nb = tq // Q_BLOCK
    qb = jnp.moveaxis(q.reshape(b, nb, Q_BLOCK, hk, grp, d), 1, 0)

    def one_block(q_blk):
        s = jnp.einsum('bqhgd,bkhd->bhgqk', q_blk, k, preferred_element_type=jnp.float32) * scale
        p = jax.nn.softmax(s, axis=-1).astype(v.dtype)
        return jnp.einsum('bhgqk,bkhd->bqhgd', p, v)

    o = lax.map(one_block, qb)
    return jnp.moveaxis(o, 0, 1).reshape(b, tq, hk, grp, v.shape[-1])


def _gqa_heads(aq, ak, av, g_q, g_k, rope):
    b, t, _ = aq.shape
    q = _rmsnorm(aq.reshape(b, t, GQA_HEADS, GQA_HEAD_DIM), g_q)
    k = _rmsnorm(ak.reshape(b, t, GQA_KV_HEADS, GQA_HEAD_DIM), g_k)
    v = av.reshape(b, t, GQA_KV_HEADS, GQA_HEAD_DIM)
    if rope is not None:
        q = _apply_axial_rope(q, rope)
        k = _apply_axial_rope(k, rope)
    q = q.reshape(b, t, GQA_KV_HEADS, GQA_HEADS // GQA_KV_HEADS, GQA_HEAD_DIM)
    return q, k, v


def _mla_heads(dq, dkv, kr, g_q, g_kv, w_uq, w_ukv, rope):
    b, t, _ = dq.shape
    q = (_rmsnorm(dq, g_q) @ w_uq).reshape(b, t, MLA_HEADS, MLA_QK_DIM)
    kv = (_rmsnorm(dkv, g_kv) @ w_ukv).reshape(b, t, MLA_HEADS, MLA_NOPE_DIM + MLA_V_DIM)
    q_nope, q_rope = q[..., :MLA_NOPE_DIM], q[..., MLA_NOPE_DIM:]
    k_nope, v = kv[..., :MLA_NOPE_DIM], kv[..., MLA_NOPE_DIM:]
    k_rope = kr[:, :, None, :]
    if rope is not None:
        q_rope = _apply_axial_rope(q_rope, rope)
        k_rope = _apply_axial_rope(k_rope, rope)
    q = jnp.concatenate([q_nope, q_rope], axis=-1)[:, :, :, None, :]
    k = jnp.concatenate([k_nope, jnp.broadcast_to(k_rope, (b, t, MLA_HEADS, MLA_ROPE_DIM))], axis=-1)
    return q, k, v


def _rwkv_prepare(p, mu, w0, w_up, a0, a_up, g_up, k_k, k_a):
    b, t, _ = p.shape
    edge = jnp.zeros_like(p[:, :1])
    prev = jnp.concatenate([edge, p[:, :-1]], axis=1)
    nxt = jnp.concatenate([p[:, 1:], edge], axis=1)
    p = p + mu[0] * (prev - p) + mu[1] * (nxt - p)
    r, k, v, wd_f, wd_b, ad_f, ad_b, gd = _split(p, RWKV_SPLITS)

    def heads(z):
        return z.astype(jnp.float32).reshape(b, t, RWKV_HEADS, RWKV_HEAD_DIM)

    g = jax.nn.sigmoid(gd) @ g_up
    kk = heads(k * k_k)
    kk = kk / jnp.maximum(jnp.linalg.norm(kk, axis=-1, keepdims=True), 1e-12)
    dirs = []
    for d, (wd, ad) in enumerate(((wd_f, ad_f), (wd_b, ad_b))):
        w_log = -jax.nn.softplus(-(w0[d] + jnp.tanh(wd) @ w_up[d]).astype(jnp.float32)) - 0.5
        decay = jnp.exp(-jnp.exp(w_log))
        a = jax.nn.sigmoid(a0[d] + ad @ a_up[d])
        k_d = k * (1 + (a - 1) * k_a)
        dirs.append((heads(decay), heads(k_d), kk * heads(a)))
    return heads(r), heads(v), g, -kk, dirs


def _wkv7_scan(state0, r, decay, k, v, a_vec, b_vec, reverse):
    xs = tuple(jnp.moveaxis(z, 1, 0) for z in (r, decay, k, v, a_vec, b_vec))

    def step(state, inp):
        r_t, w_t, k_t, v_t, a_t, b_t = inp
        sa = jnp.einsum('bhvk,bhk->bhv', state, a_t)
        state = (state * w_t[:, :, None, :] + sa[..., None] * b_t[:, :, None, :]
                 + v_t[..., None] * k_t[:, :, None, :])
        return state, jnp.einsum('bhvk,bhk->bhv', state, r_t)

    state, ys = lax.scan(step, state0, xs, reverse=reverse)
    return state, jnp.moveaxis(ys, 0, 1)


def _rwkv_scan(prep, direction, state0):
    r, v, _, neg_kk, dirs = prep
    decay, k_d, b_d = dirs[direction]
    return _wkv7_scan(state0, r, decay, k_d, v, neg_kk, b_d, reverse=(direction == 1))


def _rwkv_output(y_f, y_b, prep, r_k, gn_g, gn_b):
    r, v, g, _, dirs = prep
    b, t = r.shape[:2]
    y = y_f + y_b
    mu = jnp.mean(y, -1, keepdims=True)
    var = jnp.mean(jnp.square(y - mu), -1, keepdims=True)
    y = ((y - mu) * lax.rsqrt(var + GN_EPS)).reshape(b, t, RWKV_WIDTH) * gn_g + gn_b
    bonus = jnp.sum(r * (dirs[0][1] + dirs[1][1]) * r_k, axis=-1, keepdims=True) * v
    return ((y + bonus.reshape(b, t, RWKV_WIDTH)) * g).astype(g.dtype)


def _merge(outs, gate_cols, w_branch_up, w_o):
    gates = _split(gate_cols, GATE_SPLITS)
    m = sum(jax.nn.sigmoid(gb) * (ob @ w_branch_up[i]) for i, (ob, gb) in enumerate(zip(outs, gates)))
    return m @ w_o


def _mixer(h_lat, h_ctx, emit_ctx, w_in, gqa_q_norm, gqa_k_norm, mla_q_norm, mla_kv_norm,
           mla_w_uq, mla_w_ukv, rwkv_mu, rwkv_w0, rwkv_w_up, rwkv_a0, rwkv_a_up, rwkv_g_up,
           rwkv_k_k, rwkv_k_a, rwkv_r_k, rwkv_gn_g, rwkv_gn_b, w_branch_up, w_o):
    b, s, _ = h_lat.shape
    aq_l, ak_l, av_l, dq_l, dkv_l, kr_l, rw_l, gate_l = _split(h_lat @ w_in, IN_SPLITS)
    aq_c, ak_c, av_c, dq_c, dkv_c, kr_c, rw_c, gate_c = _split(h_ctx @ w_in, IN_SPLITS)
    rope_a = _axial_rope_tables(s, GQA_HEAD_DIM)
    rope_b = _axial_rope_tables(s, MLA_ROPE_DIM)

    qa_l, ka_l, va_l = _gqa_heads(aq_l, ak_l, av_l, gqa_q_norm, gqa_k_norm, rope_a)
    qa_c, ka_c, va_c = _gqa_heads(aq_c, ak_c, av_c, gqa_q_norm, gqa_k_norm, None)
    oa_l = _blocked_attention(qa_l, jnp.concatenate([ka_c, ka_l], 1), jnp.concatenate([va_c, va_l], 1),
                              GQA_HEAD_DIM ** -0.5).reshape(b, s, BRANCH_WIDTH)

    qb_l, kb_l, vb_l = _mla_heads(dq_l, dkv_l, kr_l, mla_q_norm, mla_kv_norm, mla_w_uq, mla_w_ukv, rope_b)
    qb_c, kb_c, vb_c = _mla_heads(dq_c, dkv_c, kr_c, mla_q_norm, mla_kv_norm, mla_w_uq, mla_w_ukv, None)
    ob_l = _blocked_attention(qb_l, jnp.concatenate([kb_c, kb_l], 1), jnp.concatenate([vb_c, vb_l], 1),
                              MLA_QK_DIM ** -0.5).reshape(b, s, BRANCH_WIDTH)

    prep_l = _rwkv_prepare(rw_l, rwkv_mu, rwkv_w0, rwkv_w_up, rwkv_a0, rwkv_a_up, rwkv_g_up, rwkv_k_k, rwkv_k_a)
    prep_c = _rwkv_prepare(rw_c, rwkv_mu, rwkv_w0, rwkv_w_up, rwkv_a0, rwkv_a_up, rwkv_g_up, rwkv_k_k, rwkv_k_a)
    zero_state = jnp.zeros((b, RWKV_HEADS, RWKV_HEAD_DIM, RWKV_HEAD_DIM), jnp.float32)
    sf_c, yf_c = _rwkv_scan(prep_c, 0, zero_state)
    sb_c, yb_c = _rwkv_scan(prep_c, 1, zero_state)
    _, yf_l = _rwkv_scan(prep_l, 0, sf_c)
    _, yb_l = _rwkv_scan(prep_l, 1, sb_c)
    oc_l = _rwkv_output(yf_l, yb_l, prep_l, rwkv_r_k, rwkv_gn_g, rwkv_gn_b)

    m_lat = _merge((oa_l, ob_l, oc_l), gate_l, w_branch_up, w_o)
    if not emit_ctx:
        return m_lat, None
    lc = h_ctx.shape[1]
    oa_c = _blocked_attention(qa_c, ka_c, va_c, GQA_HEAD_DIM ** -0.5).reshape(b, lc, BRANCH_WIDTH)
    ob_c = _blocked_attention(qb_c, kb_c, vb_c, MLA_QK_DIM ** -0.5).reshape(b, lc, BRANCH_WIDTH)
    oc_c = _rwkv_output(yf_c, yb_c, prep_c, rwkv_r_k, rwkv_gn_g, rwkv_gn_b)
    m_ctx = _merge((oa_c, ob_c, oc_c), gate_c, w_branch_up, w_o)
    return m_lat, m_ctx


def _moe(h, router_w, router_b, w1, b1, w2, b2):
    n_tok, d = h.shape
    n_assign = n_tok * TOP_K
    n_blocks = -(-(n_assign + N_EXPERTS * (DISPATCH_BLOCK - 1)) // DISPATCH_BLOCK)
    n_slots = n_blocks * DISPATCH_BLOCK
    logits = jnp.matmul(h, router_w, preferred_element_type=jnp.float32) + router_b.astype(jnp.float32)
    top_val, top_idx = lax.top_k(logits, TOP_K)
    gate = jax.nn.softmax(top_val, axis=-1).astype(h.dtype).reshape(-1)
    expert = top_idx.reshape(-1)
    token = jnp.repeat(jnp.arange(n_tok, dtype=jnp.int32), TOP_K)
    order = jnp.argsort(expert)
    expert_s, token_s, gate_s = expert[order], token[order], gate[order]
    counts = jnp.bincount(expert, length=N_EXPERTS)
    padded = (counts + DISPATCH_BLOCK - 1) // DISPATCH_BLOCK * DISPATCH_BLOCK
    start = jnp.cumsum(counts) - counts
    pad_end = jnp.cumsum(padded)
    pad_start = pad_end - padded
    slot = pad_start[expert_s] + jnp.arange(n_assign) - start[expert_s]
    slot_token = jnp.full((n_slots,), n_tok, jnp.int32).at[slot].set(token_s)
    slot_gate = jnp.zeros((n_slots,), h.dtype).at[slot].set(gate_s)
    block_expert = jnp.minimum(
        jnp.searchsorted(pad_end, jnp.arange(n_blocks) * DISPATCH_BLOCK, side='right'), N_EXPERTS - 1)
    h_pad = jnp.concatenate([h, jnp.zeros((1, d), h.dtype)], axis=0)

    def expert_block(args):
        tok, g, e = args
        u = h_pad[tok] @ w1[e] + b1[e]
        u_glu = jnp.minimum(u[:, :EXPERT_FF], SWIGLU_LIMIT)
        u_lin = jnp.clip(u[:, EXPERT_FF:], -SWIGLU_LIMIT, SWIGLU_LIMIT)
        act = u_glu * jax.nn.sigmoid(SWIGLU_ALPHA * u_glu) * (u_lin + 1)
        return (act @ w2[e] + b2[e]) * g[:, None]

    y = lax.map(expert_block, (slot_token.reshape(n_blocks, DISPATCH_BLOCK),
                               slot_gate.reshape(n_blocks, DISPATCH_BLOCK), block_expert))
    out = jnp.zeros((n_tok + 1, d), h.dtype).at[slot_token].add(y.reshape(n_slots, d))
    return out[:n_tok]


def setup_inputs(seed: int = 0) -> dict:
    key = jax.random.key(seed)
    ks = iter(jax.random.split(key, 48))

    def nrm(shape, scale=1.0):
        return scale * jax.random.normal(next(ks), shape, jnp.float32)

    def gain(shape):
        return 1.0 + nrm(shape, 0.02)

    L, D, W, E, F = DEPTH, D_MODEL, RWKV_WIDTH, N_EXPERTS, EXPERT_FF
    return {
        'x': nrm((BATCH, SEQ, D)),
        'c': nrm((BATCH, D)),
        'ctx': nrm((BATCH, CTX_LEN, D)),
        'c_ctx': nrm((D,)),
        'w_ada': nrm((L, D, 6 * D), 0.5 * D ** -0.5),
        'b_ada': nrm((L, 6 * D), 0.02),
        'w_in': nrm((L, D, IN_WIDTH), D ** -0.5),
        'gqa_q_norm': gain((L, GQA_HEAD_DIM)),
        'gqa_k_norm': gain((L, GQA_HEAD_DIM)),
        'mla_q_norm': gain((L, MLA_Q_RANK)),
        'mla_kv_norm': gain((L, MLA_KV_RANK)),
        'mla_w_uq': nrm((L, MLA_Q_RANK, MLA_HEADS * MLA_QK_DIM), MLA_Q_RANK ** -0.5),
        'mla_w_ukv': nrm((L, MLA_KV_RANK, MLA_HEADS * (MLA_NOPE_DIM + MLA_V_DIM)), MLA_KV_RANK ** -0.5),
        'rwkv_mu': 0.25 + nrm((L, 2, RWKV_IN), 0.05),
        'rwkv_w0': -0.5 + nrm((L, 2, W), 0.5),
        'rwkv_w_up': nrm((L, 2, RWKV_DECAY_RANK, W), 0.5 * RWKV_DECAY_RANK ** -0.5),
        'rwkv_a0': nrm((L, 2, W), 0.1),
        'rwkv_a_up': nrm((L, 2, RWKV_ICLR_RANK, W), 0.5 * RWKV_ICLR_RANK ** -0.5),
        'rwkv_g_up': nrm((L, RWKV_GATE_RANK, W), RWKV_GATE_RANK ** -0.5),
        'rwkv_k_k': 0.85 + nrm((L, W), 0.02),
        'rwkv_k_a': 1.0 + nrm((L, W), 0.02),
        'rwkv_r_k': nrm((L, RWKV_HEADS, RWKV_HEAD_DIM), 0.1),
        'rwkv_gn_g': gain((L, W)),
        'rwkv_gn_b': nrm((L, W), 0.02),
        'w_branch_up': nrm((L, N_BRANCHES, BRANCH_WIDTH, D), BRANCH_WIDTH ** -0.5),
        'w_o': nrm((L, D, D), DEEPNORM_BETA * D ** -0.5),
        'ln1_g': gain((L, D)),
        'ln1_b': nrm((L, D), 0.02),
        'router_w': nrm((L, D, E), D ** -0.5),
        'router_b': nrm((L, E), 0.01),
        'exp_w_in': nrm((L, E, D, 2 * F), D ** -0.5),
        'exp_b_in': nrm((L, E, 2 * F), 0.01),
        'exp_w_out': nrm((L, E, F, D), DEEPNORM_BETA * F ** -0.5),
        'exp_b_out': nrm((L, E, D), 0.01),
        'ln2_g': gain((L, D)),
        'ln2_b': nrm((L, D), 0.02),
    }


def reference(x, c, ctx, c_ctx, w_ada, b_ada, w_in, gqa_q_norm, gqa_k_norm, mla_q_norm, mla_kv_norm,
              mla_w_uq, mla_w_ukv, rwkv_mu, rwkv_w0, rwkv_w_up, rwkv_a0, rwkv_a_up, rwkv_g_up,
              rwkv_k_k, rwkv_k_a, rwkv_r_k, rwkv_gn_g, rwkv_gn_b, w_branch_up, w_o, ln1_g, ln1_b,
              router_w, router_b, exp_w_in, exp_b_in, exp_w_out, exp_b_out, ln2_g, ln2_b):
    n_lat = x.shape[0] * x.shape[1]
    silu_c = jax.nn.silu(c)
    silu_cc = jax.nn.silu(c_ctx)
    x_lat, x_ctx = x, ctx
    for l in range(DEPTH):
        emit_ctx = l < DEPTH - 1
        sh1, sc1, g1, sh2, sc2, g2 = (m[:, None, :] for m in
                                      _split(silu_c @ w_ada[l] + b_ada[l], (D_MODEL,) * 6))
        csh1, csc1, cg1, csh2, csc2, cg2 = _split(silu_cc @ w_ada[l] + b_ada[l], (D_MODEL,) * 6)
        m_lat, m_ctx = _mixer(
            x_lat * (1 + sc1) + sh1, x_ctx * (1 + csc1) + csh1, emit_ctx, w_in[l],
            gqa_q_norm[l], gqa_k_norm[l], mla_q_norm[l], mla_kv_norm[l], mla_w_uq[l], mla_w_ukv[l],
            rwkv_mu[l], rwkv_w0[l], rwkv_w_up[l], rwkv_a0[l], rwkv_a_up[l], rwkv_g_up[l],
            rwkv_k_k[l], rwkv_k_a[l], rwkv_r_k[l], rwkv_gn_g[l], rwkv_gn_b[l], w_branch_up[l], w_o[l])
        x_lat = _layernorm(DEEPNORM_ALPHA * x_lat + g1 * m_lat, ln1_g[l], ln1_b[l])
        tokens = (x_lat * (1 + sc2) + sh2).reshape(n_lat, D_MODEL)
        if emit_ctx:
            x_ctx = _layernorm(DEEPNORM_ALPHA * x_ctx + cg1 * m_ctx, ln1_g[l], ln1_b[l])
            tokens = jnp.concatenate([tokens, (x_ctx * (1 + csc2) + csh2).reshape(-1, D_MODEL)], axis=0)
        f = _moe(tokens, router_w[l], router_b[l], exp_w_in[l], exp_b_in[l], exp_w_out[l], exp_b_out[l])
        x_lat = _layernorm(DEEPNORM_ALPHA * x_lat + g2 * f[:n_lat].reshape(x_lat.shape), ln2_g[l], ln2_b[l])
        if emit_ctx:
            x_ctx = _layernorm(DEEPNORM_ALPHA * x_ctx + cg2 * f[n_lat:].reshape(x_ctx.shape), ln2_g[l], ln2_b[l])
    return x_lat
```

```python
import functools

import jax
import jax.numpy as jnp
import numpy as np
from jax import lax
from jax.experimental import pallas as pl
from jax.experimental.pallas import tpu as pltpu

F32, BF16 = jnp.float32, jnp.bfloat16

D_MODEL = 4096
SEQ = 8192
CTX_LEN = 256
N_TOK = CTX_LEN + SEQ
DEPTH = 2
GRID_W = 64
ROPE_THETA = 10000.0

GQA_HEADS, GQA_KV_HEADS, GQA_HEAD_DIM = 8, 2, 128
MLA_HEADS, MLA_Q_RANK, MLA_KV_RANK = 8, 896, 512
MLA_NOPE_DIM, MLA_ROPE_DIM, MLA_V_DIM = 128, 64, 128
MLA_QK_DIM = MLA_NOPE_DIM + MLA_ROPE_DIM
RWKV_HEADS, RWKV_HEAD_DIM = 16, 64
RWKV_WIDTH = RWKV_HEADS * RWKV_HEAD_DIM
RWKV_DECAY_RANK, RWKV_ICLR_RANK, RWKV_GATE_RANK = 128, 128, 480
RWKV_SPLITS = (RWKV_WIDTH, RWKV_WIDTH, RWKV_WIDTH, RWKV_DECAY_RANK, RWKV_DECAY_RANK,
               RWKV_ICLR_RANK, RWKV_ICLR_RANK, RWKV_GATE_RANK)
RWKV_IN = sum(RWKV_SPLITS)
BRANCH_WIDTH = 1024
N_BRANCHES = 3
N_EXPERTS, TOP_K, EXPERT_FF = 32, 4, 512
SWIGLU_ALPHA, SWIGLU_LIMIT = 1.702, 7.0
DISPATCH_BLOCK = 128
DEEPNORM_ALPHA = (2 * DEPTH) ** 0.25
LN_EPS, RMS_EPS, GN_EPS = 1e-5, 1e-6, 64e-5

IN_SPLITS = (1024, 256, 256, MLA_Q_RANK, MLA_KV_RANK, MLA_ROPE_DIM, RWKV_IN, N_BRANCHES * D_MODEL)
IN_WIDTH = sum(IN_SPLITS)

ROW_TILE = 256
WKV_CHUNK = 64
VMEM_LIMIT = 48 * 1024 * 1024


def _params(*sem):
    return pltpu.CompilerParams(dimension_semantics=sem, vmem_limit_bytes=VMEM_LIMIT)


def _mm_kernel(a_ref, b_ref, bias_ref, o_ref, acc_ref, *, nk):
    k = pl.program_id(2)

    @pl.when(k == 0)
    def _():
        acc_ref[...] = jnp.zeros_like(acc_ref)

    acc_ref[...] += jnp.dot(a_ref[...].astype(BF16), b_ref[...].astype(BF16), preferred_element_type=F32)

    @pl.when(k == nk - 1)
    def _():
        o_ref[...] = (acc_ref[...] + bias_ref[...]).astype(o_ref.dtype)


def _pick(n, prefs):
    for p in prefs:
        if n % p == 0:
            return p
    return n


def matmul(a, b, bias=None, out_dtype=F32, tm=None, tn=None, tk=None, name="matmul"):
    m, kdim = a.shape
    _, n = b.shape
    tm = tm or _pick(m, (768, 512, 256, 128, 8))
    tn = tn or _pick(n, (512, 256, 128))
    tk = tk or _pick(kdim, (1024, 512, 256, 128))
    nk = kdim // tk
    if bias is None:
        bias = jnp.zeros((n,), F32)
    bias = bias.reshape(1, n).astype(F32)
    return pl.pallas_call(
        functools.partial(_mm_kernel, nk=nk),
        grid=(m // tm, n // tn, nk),
        in_specs=[pl.BlockSpec((tm, tk), lambda i, j, k: (i, k)),
                  pl.BlockSpec((tk, tn), lambda i, j, k: (k, j)),
                  pl.BlockSpec((1, tn), lambda i, j, k: (0, j))],
        out_specs=pl.BlockSpec((tm, tn), lambda i, j, k: (i, j)),
        out_shape=jax.ShapeDtypeStruct((m, n), out_dtype),
        scratch_shapes=[pltpu.VMEM((tm, tn), F32)],
        compiler_params=_params("parallel", "parallel", "arbitrary"),
        name=name,
    )(a, b, bias)


def _router_kernel(x_ref, w_ref, b_ref, o_ref):
    x = x_ref[...]
    w = w_ref[...]
    xh = x.astype(BF16)
    xl = (x - xh.astype(F32)).astype(BF16)
    wh = w.astype(BF16)
    wl = (w - wh.astype(F32)).astype(BF16)
    acc = jnp.dot(xh, wh, preferred_element_type=F32)
    acc += jnp.dot(xh, wl, preferred_element_type=F32)
    acc += jnp.dot(xl, wh, preferred_element_type=F32)
    o_ref[...] = acc + b_ref[...]


def router_logits(tokens, router_w, router_b):
    n_tok = tokens.shape[0]
    npad = 128
    w = jnp.zeros((D_MODEL, npad), F32).at[:, :N_EXPERTS].set(router_w)
    b = jnp.zeros((1, npad), F32).at[0, :N_EXPERTS].set(router_b)
    out = pl.pallas_call(
        _router_kernel,
        grid=(n_tok // ROW_TILE,),
        in_specs=[pl.BlockSpec((ROW_TILE, D_MODEL), lambda i: (i, 0)),
                  pl.BlockSpec((D_MODEL, npad), lambda i: (0, 0)),
                  pl.BlockSpec((1, npad), lambda i: (0, 0))],
        out_specs=pl.BlockSpec((ROW_TILE, npad), lambda i: (i, 0)),
        out_shape=jax.ShapeDtypeStruct((n_tok, npad), F32),
        compiler_params=_params("parallel"),
        name="router",
    )(tokens, w, b)
    return out[:, :N_EXPERTS]


def _mod_index(i):
    return (jnp.minimum(i, 1), 0, 0)


def _modulate_kernel(x_ref, sc_ref, sh_ref, o_ref):
    o_ref[...] = (x_ref[...] * (1.0 + sc_ref[0]) + sh_ref[0]).astype(o_ref.dtype)


def modulate(x, sc, sh, out_dtype=BF16):
    n = x.shape[0]
    return pl.pallas_call(
        _modulate_kernel,
        grid=(n // ROW_TILE,),
        in_specs=[pl.BlockSpec((ROW_TILE, D_MODEL), lambda i: (i, 0)),
                  pl.BlockSpec((1, 1, D_MODEL), _mod_index),
                  pl.BlockSpec((1, 1, D_MODEL), _mod_index)],
        out_specs=pl.BlockSpec((ROW_TILE, D_MODEL), lambda i: (i, 0)),
        out_shape=jax.ShapeDtypeStruct((n, D_MODEL), out_dtype),
        compiler_params=_params("parallel"),
        name="modulate",
    )(x, sc, sh)


def _postln_kernel(x_ref, m_ref, g_ref, sc_ref, sh_ref, lng_ref, lnb_ref, y_ref, t_ref):
    z = DEEPNORM_ALPHA * x_ref[...] + g_ref[0] * m_ref[...]
    mu = jnp.mean(z, axis=-1, keepdims=True)
    zc = z - mu
    var = jnp.mean(zc * zc, axis=-1, keepdims=True)
    y = zc * lax.rsqrt(var + LN_EPS) * lng_ref[...] + lnb_ref[...]
    y_ref[...] = y
    t_ref[...] = y * (1.0 + sc_ref[0]) + sh_ref[0]


def post_ln(x, m, g, sc, sh, ln_g, ln_b):
    n = x.shape[0]
    row = pl.BlockSpec((ROW_TILE, D_MODEL), lambda i: (i, 0))
    mod = pl.BlockSpec((1, 1, D_MODEL), _mod_index)
    vec = pl.BlockSpec((1, D_MODEL), lambda i: (0, 0))
    return pl.pallas_call(
        _postln_kernel,
        grid=(n // ROW_TILE,),
        in_specs=[row, row, mod, mod, mod, vec, vec],
        out_specs=[row, row],
        out_shape=[jax.ShapeDtypeStruct((n, D_MODEL), F32)] * 2,
        compiler_params=_params("parallel"),
        name="post_ln",
    )(x, m, g, sc, sh, ln_g.reshape(1, D_MODEL), ln_b.reshape(1, D_MODEL))


def _attn_kernel(*refs, nkv, two_part):
    if two_part:
        q_ref, q2_ref, k_ref, k2_ref, v_ref, o_ref, m_ref, l_ref, acc_ref = refs
    else:
        q_ref, k_ref, v_ref, o_ref, m_ref, l_ref, acc_ref = refs
    ki = pl.program_id(2)
    g, tq, d = q_ref.shape[1:]

    @pl.when(ki == 0)
    def _():
        m_ref[...] = jnp.full_like(m_ref, -jnp.inf)
        l_ref[...] = jnp.zeros_like(l_ref)
        acc_ref[...] = jnp.zeros_like(acc_ref)

    q = q_ref[0].reshape(g * tq, d)
    s = lax.dot_general(q, k_ref[0], (((1,), (1,)), ((), ())), preferred_element_type=F32)
    if two_part:
        q2 = q2_ref[0].reshape(g * tq, q2_ref.shape[-1])
        s += lax.dot_general(q2, k2_ref[0], (((1,), (1,)), ((), ())), preferred_element_type=F32)
    m_prev = m_ref[...]
    m_new = jnp.maximum(m_prev, jnp.max(s, axis=-1, keepdims=True))
    alpha = jnp.exp(m_prev - m_new)
    p = jnp.exp(s - m_new)
    l_ref[...] = alpha * l_ref[...] + jnp.sum(p, axis=-1, keepdims=True)
    acc_ref[...] = alpha * acc_ref[...] + jnp.dot(p.astype(BF16), v_ref[0], preferred_element_type=F32)
    m_ref[...] = m_new

    @pl.when(ki == nkv - 1)
    def _():
        o = acc_ref[...] / l_ref[...]
        o_ref[0] = o.reshape(g, tq, o.shape[-1]).astype(o_ref.dtype)


def attention(q, k, v, q2=None, k2=None, tq=256, tk=768, name="attn"):
    hkv, g, tq_all, d = q.shape
    tk_all = k.shape[1]
    dv = v.shape[-1]
    tq = min(tq, tq_all)
    tk = min(tk, tk_all)
    nkv = tk_all // tk
    two_part = q2 is not None
    in_specs = [pl.BlockSpec((1, g, tq, d), lambda h, i, j: (h, 0, i, 0))]
    args = [q]
    if two_part:
        in_specs.append(pl.BlockSpec((1, g, tq, q2.shape[-1]), lambda h, i, j: (h, 0, i, 0)))
        args.append(q2)
    in_specs.append(pl.BlockSpec((1, tk, d), lambda h, i, j: (h, j, 0)))
    args.append(k)
    if two_part:
        in_specs.append(pl.BlockSpec((1, tk, k2.shape[-1]), lambda h, i, j: (0, j, 0)))
        args.append(k2)
    in_specs.append(pl.BlockSpec((1, tk, dv), lambda h, i, j: (h, j, 0)))
    args.append(v)
    return pl.pallas_call(
        functools.partial(_attn_kernel, nkv=nkv, two_part=two_part),
        grid=(hkv, tq_all // tq, nkv),
        in_specs=in_specs,
        out_specs=pl.BlockSpec((1, g, tq, dv), lambda h, i, j: (h, 0, i, 0)),
        out_shape=jax.ShapeDtypeStruct((hkv, g, tq_all, dv), F32),
        scratch_shapes=[pltpu.VMEM((g * tq, 1), F32), pltpu.VMEM((g * tq, 1), F32),
                        pltpu.VMEM((g * tq, dv), F32)],
        compiler_params=_params("parallel", "parallel", "arbitrary"),
        name=name,
    )(*args)


def _dot(a, b):
    return jnp.dot(a.astype(BF16), b.astype(BF16), preferred_element_type=F32)


def _dot_t(a, b):
    return lax.dot_general(a.astype(BF16), b.astype(BF16), (((1,), (1,)), ((), ())), preferred_element_type=F32)


def _tdot(a, b):
    return lax.dot_general(a.astype(BF16), b.astype(BF16), (((0,), (0,)), ((), ())), preferred_element_type=F32)


def _cumsum_f32(tri, x):
    x1 = x.astype(BF16)
    r1 = x - x1.astype(F32)
    x2 = r1.astype(BF16)
    x3 = (r1 - x2.astype(F32)).astype(BF16)
    t = tri.astype(BF16)
    return (jnp.dot(t, x1, preferred_element_type=F32) + jnp.dot(t, x2, preferred_element_type=F32)
            + jnp.dot(t, x3, preferred_element_type=F32))


def _wkv_kernel(r_ref, lw_ref, k_ref, v_ref, a_ref, b_ref, y_ref, s_ref, *, reverse, heads):
    c = WKV_CHUNK
    n = RWKV_HEAD_DIM
    step = pl.program_id(1)

    @pl.when(step == 0)
    def _():
        s_ref[...] = jnp.zeros_like(s_ref)

    row = lax.broadcasted_iota(jnp.int32, (c, c), 0)
    col = lax.broadcasted_iota(jnp.int32, (c, c), 1)
    if reverse:
        incl = col >= row
        strict = col > row
    else:
        incl = col <= row
        strict = col < row
    incl_f = incl.astype(F32)
    eye = (row == col).astype(F32)

    lw = lw_ref[...]
    cum = _cumsum_f32(incl_f, lw)
    p_incl = jnp.exp(cum)
    p_excl = jnp.exp(cum - lw)
    p_inv = jnp.exp(-cum)
    last = 0 if reverse else c - 1
    p_end = p_incl[last:last + 1, :]

    at_all = a_ref[...] * p_excl
    rt_all = r_ref[...] * p_incl
    bt_all = b_ref[...] * p_inv
    kt_all = k_ref[...] * p_inv
    v_all = v_ref[...]

    for h in range(heads):
        sl = slice(h * n, (h + 1) * n)
        at, rt, bt, kt, v = at_all[:, sl], rt_all[:, sl], bt_all[:, sl], kt_all[:, sl], v_all[:, sl]
        pe = p_end[:, sl]
        s0 = s_ref[h]
        l_ab = jnp.where(strict, _dot_t(at, bt), 0.0)
        l_ak = jnp.where(strict, _dot_t(at, kt), 0.0)
        m_rb = jnp.where(incl, _dot_t(rt, bt), 0.0)
        m_rk = jnp.where(incl, _dot_t(rt, kt), 0.0)
        t_inv = eye + l_ab
        l_pow = l_ab
        span = 1
        while span * 2 < c:
            l_pow = _dot(l_pow, l_pow)
            t_inv = _dot(t_inv, eye + l_pow)
            span *= 2
        rhs = _dot_t(at, s0) + _dot(l_ak, v)
        sa = _dot(t_inv, rhs)
        y = _dot_t(rt, s0) + _dot(m_rb, sa) + _dot(m_rk, v)
        y_ref[:, sl] = y
        s_ref[h] = s0 * pe + _tdot(sa, bt * pe) + _tdot(v, kt * pe)


def wkv_scan(r, lw, k, v, a, b, reverse, heads_per_step=4):
    c = WKV_CHUNK
    n_chunks = N_TOK // c
    ctx_chunks = CTX_LEN // c
    width = heads_per_step * RWKV_HEAD_DIM

    if reverse:
        def chunk_of(s):
            return jnp.where(s < ctx_chunks, ctx_chunks - 1 - s, n_chunks - 1 + ctx_chunks - s)
    else:
        def chunk_of(s):
            return s

    spec = pl.BlockSpec((c, width), lambda g, s: (chunk_of(s), g))
    return pl.pallas_call(
        functools.partial(_wkv_kernel, reverse=reverse, heads=heads_per_step),
        grid=(RWKV_HEADS // heads_per_step, n_chunks),
        in_specs=[spec] * 6,
        out_specs=spec,
        out_shape=jax.ShapeDtypeStruct((N_TOK, RWKV_WIDTH), F32),
        scratch_shapes=[pltpu.VMEM((heads_per_step, RWKV_HEAD_DIM, RWKV_HEAD_DIM), F32)],
        compiler_params=_params("parallel", "arbitrary"),
        name="wkv_bwd" if reverse else "wkv_fwd",
    )(r, lw, k, v, a, b)


def _moe_kernel(be_ref, x_ref, w1_ref, b1_ref, w2_ref, b2_ref, g_ref, o_ref):
    del be_ref
    u = jnp.dot(x_ref[...], w1_ref[0], preferred_element_type=F32) + b1_ref[0]
    u_glu = jnp.minimum(u[:, :EXPERT_FF], SWIGLU_LIMIT)
    u_lin = jnp.clip(u[:, EXPERT_FF:], -SWIGLU_LIMIT, SWIGLU_LIMIT)
    act = u_glu * jax.nn.sigmoid(SWIGLU_ALPHA * u_glu) * (u_lin + 1.0)
    y = jnp.dot(act.astype(BF16), w2_ref[0], preferred_element_type=F32) + b2_ref[0]
    o_ref[...] = y * g_ref[...]


def moe_experts(xg, slot_gate, block_expert, w1, b1, w2, b2):
    n_slots = xg.shape[0]
    n_blocks = n_slots // DISPATCH_BLOCK
    grid_spec = pltpu.PrefetchScalarGridSpec(
        num_scalar_prefetch=1,
        grid=(n_blocks,),
        in_specs=[pl.BlockSpec((DISPATCH_BLOCK, D_MODEL), lambda i, be: (i, 0)),
                  pl.BlockSpec((1, D_MODEL, 2 * EXPERT_FF), lambda i, be: (be[i], 0, 0)),
                  pl.BlockSpec((1, 1, 2 * EXPERT_FF), lambda i, be: (be[i], 0, 0)),
                  pl.BlockSpec((1, EXPERT_FF, D_MODEL), lambda i, be: (be[i], 0, 0)),
                  pl.BlockSpec((1, 1, D_MODEL), lambda i, be: (be[i], 0, 0)),
                  pl.BlockSpec((DISPATCH_BLOCK, 1), lambda i, be: (i, 0))],
        out_specs=pl.BlockSpec((DISPATCH_BLOCK, D_MODEL), lambda i, be: (i, 0)),
    )
    return pl.pallas_call(
        _moe_kernel,
        grid_spec=grid_spec,
        out_shape=jax.ShapeDtypeStruct((n_slots, D_MODEL), F32),
        compiler_params=_params("arbitrary"),
        name="moe_experts",
    )(block_expert, xg, w1, b1.reshape(N_EXPERTS, 1, -1), w2, b2.reshape(N_EXPERTS, 1, -1),
      slot_gate.reshape(n_slots, 1))


def moe(tokens, router_w, router_b, w1, b1, w2, b2):
    n_tok = tokens.shape[0]
    n_assign = n_tok * TOP_K
    n_blocks = -(-(n_assign + N_EXPERTS * (DISPATCH_BLOCK - 1)) // DISPATCH_BLOCK)
    n_slots = n_blocks * DISPATCH_BLOCK
    logits = router_logits(tokens, router_w, router_b)
    top_val, top_idx = lax.top_k(logits, TOP_K)
    gate = jax.nn.softmax(top_val, axis=-1).reshape(-1)
    expert = top_idx.reshape(-1)
    token = jnp.repeat(jnp.arange(n_tok, dtype=jnp.int32), TOP_K)
    order = jnp.argsort(expert)
    expert_s, token_s, gate_s = expert[order], token[order], gate[order]
    counts = jnp.bincount(expert, length=N_EXPERTS)
    padded = (counts + DISPATCH_BLOCK - 1) // DISPATCH_BLOCK * DISPATCH_BLOCK
    start = jnp.cumsum(counts) - counts
    pad_end = jnp.cumsum(padded)
    pad_start = pad_end - padded
    slot = (pad_start[expert_s] + jnp.arange(n_assign) - start[expert_s]).astype(jnp.int32)
    slot_token = jnp.zeros((n_slots,), jnp.int32).at[slot].set(token_s)
    slot_gate = jnp.zeros((n_slots,), F32).at[slot].set(gate_s)
    block_expert = jnp.minimum(
        jnp.searchsorted(pad_end, jnp.arange(n_blocks) * DISPATCH_BLOCK, side='right'),
        N_EXPERTS - 1).astype(jnp.int32)
    slot_of = jnp.zeros((n_assign,), jnp.int32).at[order].set(slot).reshape(n_tok, TOP_K)
    xg = tokens.astype(BF16)[slot_token]
    y = moe_experts(xg, slot_gate, block_expert, w1.astype(BF16), b1, w2.astype(BF16), b2)
    return jnp.sum(y[slot_of], axis=1)


def _split(x, sizes):
    return jnp.split(x, np.cumsum(sizes)[:-1].tolist(), axis=-1)


def _rmsnorm(x, g):
    return x * lax.rsqrt(jnp.mean(jnp.square(x), -1, keepdims=True) + RMS_EPS) * g


def _rope_tables(rot_dim):
    n_freq = rot_dim // 4
    row = jnp.repeat(jnp.arange(SEQ // GRID_W, dtype=F32), GRID_W)
    col = (jnp.arange(SEQ) % GRID_W).astype(F32)
    freqs = ROPE_THETA ** (-jnp.arange(n_freq, dtype=F32) / n_freq)
    ang = jnp.stack([row[:, None] * freqs, col[:, None] * freqs], axis=1)
    ang = jnp.concatenate([jnp.zeros((CTX_LEN, 2, n_freq), F32), ang], axis=0)
    return jnp.cos(ang), jnp.sin(ang)


def _rope(x, rope):
    cos, sin = rope
    t, h, dim = x.shape
    n_freq = dim // 4
    xf = x.reshape(t, h, 2, 2, n_freq)
    x1, x2 = xf[..., 0, :], xf[..., 1, :]
    cs, sn = cos[:, None], sin[:, None]
    out = jnp.stack([x1 * cs - x2 * sn, x2 * cs + x1 * sn], axis=-2)
    return out.reshape(t, h, dim)


def _lat_ctx_attention(q, k, v, q2=None, k2=None, name="attn"):
    sl_c = lambda z: None if z is None else z[..., :CTX_LEN, :]
    sl_l = lambda z: None if z is None else z[..., CTX_LEN:, :]
    o_l = attention(sl_l(q), k, v, sl_l(q2), k2, name=name + "_lat")
    o_c = attention(sl_c(q), sl_c(k), sl_c(v), sl_c(q2), sl_c(k2), name=name + "_ctx")
    return jnp.concatenate([o_c, o_l], axis=2)


def _heads_out(o):
    hkv, g, t, dv = o.shape
    return jnp.transpose(o, (2, 0, 1, 3)).reshape(t, hkv * g * dv)


def _shift_mix(p, mu):
    def one(z):
        edge = jnp.zeros_like(z[:1])
        prev = jnp.concatenate([edge, z[:-1]], axis=0)
        nxt = jnp.concatenate([z[1:], edge], axis=0)
        return z + mu[0] * (prev - z) + mu[1] * (nxt - z)
    return jnp.concatenate([one(p[:CTX_LEN]), one(p[CTX_LEN:])], axis=0)


def _mixer(h, w_in, gqa_q_norm, gqa_k_norm, mla_q_norm, mla_kv_norm, mla_w_uq, mla_w_ukv,
           rwkv_mu, rwkv_w0, rwkv_w_up, rwkv_a0, rwkv_a_up, rwkv_g_up, rwkv_k_k, rwkv_k_a, rwkv_r_k,
           rwkv_gn_g, rwkv_gn_b, w_branch_up, w_o):
    t = N_TOK
    in_pad = -IN_WIDTH % 512
    proj = matmul(h, jnp.pad(w_in, ((0, 0), (0, in_pad))), name="in_proj")[:, :IN_WIDTH]
    aq, ak, av, dq, dkv, kr, rw, gate_cols = _split(proj, IN_SPLITS)
    rope_a = _rope_tables(GQA_HEAD_DIM)
    rope_b = _rope_tables(MLA_ROPE_DIM)

    q = _rope(_rmsnorm(aq.reshape(t, GQA_HEADS, GQA_HEAD_DIM), gqa_q_norm), rope_a) * GQA_HEAD_DIM ** -0.5
    k = _rope(_rmsnorm(ak.reshape(t, GQA_KV_HEADS, GQA_HEAD_DIM), gqa_k_norm), rope_a)
    v = av.reshape(t, GQA_KV_HEADS, GQA_HEAD_DIM)
    grp = GQA_HEADS // GQA_KV_HEADS
    q = jnp.transpose(q.reshape(t, GQA_KV_HEADS, grp, GQA_HEAD_DIM), (1, 2, 0, 3)).astype(BF16)
    k = jnp.transpose(k, (1, 0, 2)).astype(BF16)
    v = jnp.transpose(v, (1, 0, 2)).astype(BF16)
    oa = _heads_out(_lat_ctx_attention(q, k, v, name="gqa"))

    qb = matmul(_rmsnorm(dq, mla_q_norm).astype(BF16), mla_w_uq, name="mla_uq").reshape(t, MLA_HEADS, MLA_QK_DIM)
    kvb = matmul(_rmsnorm(dkv, mla_kv_norm).astype(BF16), mla_w_ukv, name="mla_ukv")
    kvb = kvb.reshape(t, MLA_HEADS, MLA_NOPE_DIM + MLA_V_DIM)
    scale = MLA_QK_DIM ** -0.5
    q_nope = qb[..., :MLA_NOPE_DIM] * scale
    q_rope = _rope(qb[..., MLA_NOPE_DIM:], rope_b) * scale
    k_nope, vb = kvb[..., :MLA_NOPE_DIM], kvb[..., MLA_NOPE_DIM:]
    k_rope = _rope(kr[:, None, :], rope_b)
    to_h = lambda z: jnp.transpose(z, (1, 0, 2)).astype(BF16)
    ob = _heads_out(_lat_ctx_attention(to_h(q_nope)[:, None], to_h(k_nope), to_h(vb),
                                       to_h(q_rope)[:, None], to_h(k_rope), name="mla"))

    p = _shift_mix(rw, rwkv_mu)
    r, kx, vx, wd_f, wd_b, ad_f, ad_b, gd = _split(p, RWKV_SPLITS)
    gd_pad = jnp.pad(jax.nn.sigmoid(gd), ((0, 0), (0, 512 - RWKV_GATE_RANK)))
    g_up_pad = jnp.pad(rwkv_g_up, ((0, 512 - RWKV_GATE_RANK), (0, 0)))
    g = matmul(gd_pad.astype(BF16), g_up_pad, name="rwkv_gate")
    heads = lambda z: z.reshape(t, RWKV_HEADS, RWKV_HEAD_DIM)
    kk = heads(kx * rwkv_k_k)
    kk = (kk / jnp.maximum(jnp.linalg.norm(kk, axis=-1, keepdims=True), 1e-12)).reshape(t, RWKV_WIDTH)
    ys, k_sum = [], 0.0
    for d, (wd, ad) in enumerate(((wd_f, ad_f), (wd_b, ad_b))):
        w_lin = matmul(jnp.tanh(wd).astype(BF16), rwkv_w_up[d], bias=rwkv_w0[d], name="rwkv_decay")
        w_log = -jax.nn.softplus(-w_lin) - 0.5
        log_decay = -jnp.exp(w_log)
        a_gate = jax.nn.sigmoid(matmul(ad.astype(BF16), rwkv_a_up[d], bias=rwkv_a0[d], name="rwkv_iclr"))
        k_d = kx * (1 + (a_gate - 1) * rwkv_k_a)
        ys.append(wkv_scan(r, log_decay, k_d, vx, -kk, kk * a_gate, reverse=(d == 1)))
        k_sum = k_sum + k_d
    y = heads(ys[0] + ys[1])
    mu = jnp.mean(y, -1, keepdims=True)
    var = jnp.mean(jnp.square(y - mu), -1, keepdims=True)
    y = ((y - mu) * lax.rsqrt(var + GN_EPS)).reshape(t, RWKV_WIDTH) * rwkv_gn_g + rwkv_gn_b
    bonus = jnp.sum(heads(r) * heads(k_sum) * rwkv_r_k, axis=-1, keepdims=True) * heads(vx)
    oc = (y + bonus.reshape(t, RWKV_WIDTH)) * g

    gates = _split(gate_cols, (D_MODEL,) * N_BRANCHES)
    m = 0.0
    for i, (ob_i, gb) in enumerate(zip((oa, ob, oc), gates)):
        m = m + jax.nn.sigmoid(gb) * matmul(ob_i.astype(BF16), w_branch_up[i], name="branch_up")
    return matmul(m.astype(BF16), w_o, name="out_proj")


def kernel(x, c, ctx, c_ctx, w_ada, b_ada, w_in, gqa_q_norm, gqa_k_norm, mla_q_norm, mla_kv_norm, mla_w_uq,
           mla_w_ukv, rwkv_mu, rwkv_w0, rwkv_w_up, rwkv_a0, rwkv_a_up, rwkv_g_up, rwkv_k_k, rwkv_k_a, rwkv_r_k,
           rwkv_gn_g, rwkv_gn_b, w_branch_up, w_o, ln1_g, ln1_b, router_w, router_b, exp_w_in, exp_b_in,
           exp_w_out, exp_b_out, ln2_g, ln2_b):
    xs = jnp.concatenate([ctx[0], x[0]], axis=0)
    cond = jnp.zeros((16, D_MODEL), F32).at[0].set(jax.nn.silu(c_ctx)).at[1].set(jax.nn.silu(c[0]))
    for l in range(DEPTH):
        ada = matmul(cond.astype(BF16), w_ada[l], bias=b_ada[l], tm=16, tn=1024, tk=1024, name="adaln")
        sh1, sc1, g1, sh2, sc2, g2 = (z[:2].reshape(2, 1, D_MODEL) for z in _split(ada, (D_MODEL,) * 6))
        h = modulate(xs, sc1, sh1)
        m = _mixer(h, w_in[l], gqa_q_norm[l], gqa_k_norm[l], mla_q_norm[l], mla_kv_norm[l], mla_w_uq[l],
                   mla_w_ukv[l], rwkv_mu[l], rwkv_w0[l], rwkv_w_up[l], rwkv_a0[l], rwkv_a_up[l], rwkv_g_up[l],
                   rwkv_k_k[l], rwkv_k_a[l], rwkv_r_k[l], rwkv_gn_g[l], rwkv_gn_b[l], w_branch_up[l], w_o[l])
        xs, tokens = post_ln(xs, m, g1, sc2, sh2, ln1_g[l], ln1_b[l])
        f = moe(tokens, router_w[l], router_b[l], exp_w_in[l], exp_b_in[l], exp_w_out[l], exp_b_out[l])
        zero = jnp.zeros((2, 1, D_MODEL), F32)
        xs, _ = post_ln(xs, f, g2, zero, zero, ln2_g[l], ln2_b[l])
    return xs[CTX_LEN:].reshape(1, SEQ, D_MODEL)
```

```python
import functools

import jax
import jax.numpy as jnp
import numpy as np
from jax import lax
from jax.experimental import pallas as pl
from jax.experimental.pallas import tpu as pltpu

F32, BF16 = jnp.float32, jnp.bfloat16

D_MODEL = 4096
SEQ = 8192
CTX_LEN = 256
N_TOK = SEQ + CTX_LEN
DEPTH = 2
GRID_W = 64
ROPE_THETA = 10000.0

GQA_HEADS, GQA_KV_HEADS, GQA_HEAD_DIM = 8, 2, 128
MLA_HEADS, MLA_Q_RANK, MLA_KV_RANK = 8, 896, 512
MLA_NOPE_DIM, MLA_ROPE_DIM, MLA_V_DIM = 128, 64, 128
MLA_QK_DIM = MLA_NOPE_DIM + MLA_ROPE_DIM
MLA_CAT_DIM = 256
RWKV_HEADS, RWKV_HEAD_DIM = 16, 64
RWKV_WIDTH = RWKV_HEADS * RWKV_HEAD_DIM
RWKV_DECAY_RANK, RWKV_ICLR_RANK, RWKV_GATE_RANK = 128, 128, 480
RWKV_SPLITS = (RWKV_WIDTH, RWKV_WIDTH, RWKV_WIDTH, RWKV_DECAY_RANK, RWKV_DECAY_RANK,
               RWKV_ICLR_RANK, RWKV_ICLR_RANK, RWKV_GATE_RANK)
RWKV_IN = sum(RWKV_SPLITS)
RWKV_IN_PAD = 4096
BRANCH_WIDTH = 1024
N_BRANCHES = 3
N_EXPERTS, TOP_K, EXPERT_FF = 32, 4, 512
SWIGLU_ALPHA, SWIGLU_LIMIT = 1.702, 7.0
DISPATCH_BLOCK = 128
DEEPNORM_ALPHA = (2 * DEPTH) ** 0.25
LN_EPS, RMS_EPS, GN_EPS = 1e-5, 1e-6, 64e-5

IN_SPLITS = (1024, 256, 256, MLA_Q_RANK, MLA_KV_RANK, MLA_ROPE_DIM, RWKV_IN, N_BRANCHES * D_MODEL)
SMALL_SPLITS = IN_SPLITS[:6]
SMALL_WIDTH = 3072

ROW_TILE = 256
LAT_TILES = SEQ // ROW_TILE
WKV_CHUNK = 64
WKV_PAIRS = RWKV_HEADS // 2
WKV_CHUNKS = N_TOK // WKV_CHUNK
WKV_CTX_CHUNKS = CTX_LEN // WKV_CHUNK
WKV_PREP_CHUNKS = 6
VMEM_LIMIT = 48 * 1024 * 1024


def _params(*sem, vmem=VMEM_LIMIT):
    return pltpu.CompilerParams(dimension_semantics=sem, vmem_limit_bytes=vmem)


def _dot(a, b):
    return jnp.dot(a.astype(BF16), b.astype(BF16), preferred_element_type=F32)


def _dot_t(a, b):
    return lax.dot_general(a.astype(BF16), b.astype(BF16), (((1,), (1,)), ((), ())), preferred_element_type=F32)


def _tdot(a, b):
    return lax.dot_general(a.astype(BF16), b.astype(BF16), (((0,), (0,)), ((), ())), preferred_element_type=F32)


def _mm_kernel(a_ref, b_ref, bias_ref, o_ref):
    o_ref[...] = (_dot(a_ref[...], b_ref[...]) + bias_ref[...]).astype(o_ref.dtype)


def _pick(n, prefs):
    for p in prefs:
        if n % p == 0:
            return p
    return n


def matmul(a, b, bias=None, out_dtype=F32, tm=None, tn=None, name="matmul"):
    m, kdim = a.shape
    _, n = b.shape
    tm = tm or _pick(m, (768, 512, 256, 128, 16))
    tn = tn or _pick(n, (512, 256, 128))
    if bias is None:
        bias = jnp.zeros((n,), F32)
    bias = bias.reshape(1, n).astype(F32)
    return pl.pallas_call(
        _mm_kernel,
        grid=(m // tm, n // tn),
        in_specs=[pl.BlockSpec((tm, kdim), lambda i, j: (i, 0)),
                  pl.BlockSpec((kdim, tn), lambda i, j: (0, j)),
                  pl.BlockSpec((1, tn), lambda i, j: (0, j))],
        out_specs=pl.BlockSpec((tm, tn), lambda i, j: (i, j)),
        out_shape=jax.ShapeDtypeStruct((m, n), out_dtype),
        compiler_params=_params("parallel", "arbitrary"),
        name=name,
    )(a, b, bias)


def _router_kernel(x_ref, w_ref, b_ref, o_ref):
    x = x_ref[...]
    w = w_ref[...]
    xh = x.astype(BF16)
    xl = (x - xh.astype(F32)).astype(BF16)
    wh = w.astype(BF16)
    wl = (w - wh.astype(F32)).astype(BF16)
    acc = jnp.dot(xh, wh, preferred_element_type=F32)
    acc += jnp.dot(xh, wl, preferred_element_type=F32)
    acc += jnp.dot(xl, wh, preferred_element_type=F32)
    o_ref[...] = acc + b_ref[...]


def router_logits(tokens, router_w, router_b):
    n_tok = tokens.shape[0]
    npad = 128
    w = jnp.zeros((D_MODEL, npad), F32).at[:, :N_EXPERTS].set(router_w)
    b = jnp.zeros((1, npad), F32).at[0, :N_EXPERTS].set(router_b)
    out = pl.pallas_call(
        _router_kernel,
        grid=(n_tok // ROW_TILE,),
        in_specs=[pl.BlockSpec((ROW_TILE, D_MODEL), lambda i: (i, 0)),
                  pl.BlockSpec((D_MODEL, npad), lambda i: (0, 0)),
                  pl.BlockSpec((1, npad), lambda i: (0, 0))],
        out_specs=pl.BlockSpec((ROW_TILE, npad), lambda i: (i, 0)),
        out_shape=jax.ShapeDtypeStruct((n_tok, npad), F32),
        compiler_params=_params("parallel"),
        name="router",
    )(tokens, w, b)
    return out[:, :N_EXPERTS]


def _mod_index(i):
    return (i // LAT_TILES, 0, 0)


def _modulate_kernel(x_ref, sc_ref, sh_ref, o_ref):
    o_ref[...] = (x_ref[...] * (1.0 + sc_ref[0]) + sh_ref[0]).astype(o_ref.dtype)


def modulate(x, sc, sh, out_dtype=BF16):
    n = x.shape[0]
    return pl.pallas_call(
        _modulate_kernel,
        grid=(n // ROW_TILE,),
        in_specs=[pl.BlockSpec((ROW_TILE, D_MODEL), lambda i: (i, 0)),
                  pl.BlockSpec((1, 1, D_MODEL), _mod_index),
                  pl.BlockSpec((1, 1, D_MODEL), _mod_index)],
        out_specs=pl.BlockSpec((ROW_TILE, D_MODEL), lambda i: (i, 0)),
        out_shape=jax.ShapeDtypeStruct((n, D_MODEL), out_dtype),
        compiler_params=_params("parallel"),
        name="modulate",
    )(x, sc, sh)


def _postln_kernel(x_ref, m_ref, g_ref, sc_ref, sh_ref, lng_ref, lnb_ref, y_ref, t_ref):
    z = DEEPNORM_ALPHA * x_ref[...] + g_ref[0] * m_ref[...]
    mu = jnp.mean(z, axis=-1, keepdims=True)
    zc = z - mu
    var = jnp.mean(zc * zc, axis=-1, keepdims=True)
    y = zc * lax.rsqrt(var + LN_EPS) * lng_ref[...] + lnb_ref[...]
    y_ref[...] = y
    t_ref[...] = y * (1.0 + sc_ref[0]) + sh_ref[0]


def post_ln(x, m, g, sc, sh, ln_g, ln_b):
    n = x.shape[0]
    row = pl.BlockSpec((ROW_TILE, D_MODEL), lambda i: (i, 0))
    mod = pl.BlockSpec((1, 1, D_MODEL), _mod_index)
    vec = pl.BlockSpec((1, D_MODEL), lambda i: (0, 0))
    return pl.pallas_call(
        _postln_kernel,
        grid=(n // ROW_TILE,),
        in_specs=[row, row, mod, mod, mod, vec, vec],
        out_specs=[row, row],
        out_shape=[jax.ShapeDtypeStruct((n, D_MODEL), F32)] * 2,
        compiler_params=_params("parallel"),
        name="post_ln",
    )(x, m, g, sc, sh, ln_g.reshape(1, D_MODEL), ln_b.reshape(1, D_MODEL))


def _attn_kernel(q_ref, k_ref, v_ref, o_ref, *, nkv, tk, unroll):
    g, tq, d = q_ref.shape[1:]
    dv = v_ref.shape[-1]
    rows = g * tq
    q = q_ref[0].reshape(rows, d)

    def body(j, carry):
        m_prev, l_prev, acc = carry
        start = pl.multiple_of(j * tk, tk)
        kj = k_ref[0, pl.ds(start, tk), :]
        vj = v_ref[0, pl.ds(start, tk), :]
        s = lax.dot_general(q, kj, (((1,), (1,)), ((), ())), preferred_element_type=F32)
        m_new = jnp.maximum(m_prev, jnp.max(s, axis=-1, keepdims=True))
        alpha = jnp.exp(m_prev - m_new)
        p = jnp.exp(s - m_new)
        l_new = alpha * l_prev + jnp.sum(p, axis=-1, keepdims=True)
        acc = alpha * acc + jnp.dot(p.astype(BF16), vj, preferred_element_type=F32)
        return m_new, l_new, acc

    init = (jnp.full((rows, 1), -jnp.inf, F32), jnp.zeros((rows, 1), F32), jnp.zeros((rows, dv), F32))
    _, l, acc = lax.fori_loop(0, nkv, body, init, unroll=unroll)
    o = acc / l
    for gi in range(g):
        o_ref[:, gi * dv:(gi + 1) * dv] = o[gi * tq:(gi + 1) * tq].astype(o_ref.dtype)


def attention(q, k, v, q_rows, kv_rows, tq, tk, unroll=2, name="attn"):
    hkv, g, _, d = q.shape
    dv = v.shape[-1]
    q0, nq = q_rows
    k0, nk = kv_rows
    assert q0 % tq == 0 and nq % tq == 0 and k0 % nk == 0 and nk % tk == 0
    qb, kb = q0 // tq, k0 // nk
    return pl.pallas_call(
        functools.partial(_attn_kernel, nkv=nk // tk, tk=tk, unroll=unroll),
        grid=(hkv, nq // tq),
        in_specs=[pl.BlockSpec((1, g, tq, d), lambda h, i: (h, 0, i + qb, 0)),
                  pl.BlockSpec((1, nk, d), lambda h, i: (h, kb, 0)),
                  pl.BlockSpec((1, nk, dv), lambda h, i: (h, kb, 0))],
        out_specs=pl.BlockSpec((tq, g * dv), lambda h, i: (i, h)),
        out_shape=jax.ShapeDtypeStruct((nq, hkv * g * dv), BF16),
        compiler_params=_params("parallel", "arbitrary"),
        name=name,
    )(q, k, v)


def lat_ctx_attention(q, k, v, tq, name):
    o_l = attention(q, k, v, (0, SEQ), (0, N_TOK), tq, 384, name=name + "_lat")
    o_c = attention(q, k, v, (SEQ, CTX_LEN), (SEQ, CTX_LEN), min(tq, CTX_LEN), CTX_LEN, unroll=1,
                    name=name + "_ctx")
    return jnp.concatenate([o_l, o_c], axis=0)


def _bd(x, m_left, m_right):
    return jnp.concatenate([x * m_left, x * m_right], axis=0)


def _cumsum_f32(tri, x):
    x1 = x.astype(BF16)
    r1 = x - x1.astype(F32)
    x2 = r1.astype(BF16)
    x3 = (r1 - x2.astype(F32)).astype(BF16)
    t = tri.astype(BF16)
    return (jnp.dot(t, x1, preferred_element_type=F32) + jnp.dot(t, x2, preferred_element_type=F32)
            + jnp.dot(t, x3, preferred_element_type=F32))


def _wkv_prep_kernel(r_ref, lw_ref, k_ref, v_ref, a_ref, b_ref,
                     wr_ref, u_ref, y0_ref, mrb_ref, bh_ref, g_ref, pc_ref, *, reverse):
    c = WKV_CHUNK
    n = RWKV_HEAD_DIM
    row = lax.broadcasted_iota(jnp.int32, (c, c), 0)
    col = lax.broadcasted_iota(jnp.int32, (c, c), 1)
    tri = ((col >= row) if reverse else (col <= row)).astype(F32)
    row2 = lax.broadcasted_iota(jnp.int32, (2 * c, 2 * c), 0) % c
    col2 = lax.broadcasted_iota(jnp.int32, (2 * c, 2 * c), 1) % c
    if reverse:
        incl, strict = col2 >= row2, col2 > row2
    else:
        incl, strict = col2 <= row2, col2 < row2
    lane = lax.broadcasted_iota(jnp.int32, (1, 2 * n), 1)
    m_left = (lane < n).astype(F32)
    m_right = 1.0 - m_left
    bd = functools.partial(_bd, m_left=m_left, m_right=m_right)
    last = 0 if reverse else c - 1

    for ci in range(WKV_PREP_CHUNKS):
        rows = slice(ci * c, (ci + 1) * c)
        lw = lw_ref[rows, :]
        cum = _cumsum_f32(tri, lw)
        p_incl = jnp.exp(cum)
        p_inv = jnp.exp(-cum)
        pe = p_incl[last:last + 1, :]
        at = bd(a_ref[rows, :] * jnp.exp(cum - lw))
        rt = bd(r_ref[rows, :] * p_incl)
        bt = bd(b_ref[rows, :] * p_inv)
        kt = bd(k_ref[rows, :] * p_inv)
        v = bd(v_ref[rows, :])
        big = _dot_t(jnp.concatenate([at, rt], axis=0), jnp.concatenate([bt, kt], axis=0))
        l_ab = jnp.where(strict, big[:2 * c, :2 * c], 0.0)
        l_ak = jnp.where(strict, big[:2 * c, 2 * c:], 0.0)
        m_rb = jnp.where(incl, big[2 * c:, :2 * c], 0.0)
        m_rk = jnp.where(incl, big[2 * c:, 2 * c:], 0.0)
        lv = _dot(jnp.concatenate([l_ak, m_rk], axis=0), v)
        x = jnp.concatenate([at, lv[:2 * c]], axis=1)
        l_pow = l_ab
        x = x + _dot(l_pow, x)
        span = 1
        while span * 2 < c:
            l_pow = _dot(l_pow, l_pow)
            x = x + _dot(l_pow, x)
            span *= 2
        wr_ref[ci, 0, :2 * c, :] = x[:, :2 * n].astype(BF16)
        wr_ref[ci, 0, 2 * c:, :] = rt.astype(BF16)
        u_ref[ci, 0] = x[:, 2 * n:]
        y0_ref[ci, 0] = lv[2 * c:]
        mrb_ref[ci, 0] = m_rb.astype(BF16)
        bh_ref[ci, 0] = (bt * pe).astype(BF16)
        g_ref[ci, 0] = _tdot(v, kt * pe)
        pc_ref[ci, 0] = jnp.broadcast_to(pe, (8, 2 * n))


def _wkv_chunk_fwd(s):
    return (s + SEQ // WKV_CHUNK) % WKV_CHUNKS


def _wkv_chunk_bwd(s):
    return WKV_CHUNKS - 1 - s


def wkv_prep(r, lw, k, v, a, b, reverse):
    c, n2 = WKV_CHUNK, 2 * RWKV_HEAD_DIM
    cb = WKV_PREP_CHUNKS
    spec = pl.BlockSpec((cb * c, n2), lambda p, i: (i, p))

    def out(rows, dtype):
        return (pl.BlockSpec((cb, 1, rows, n2), lambda p, i: (i, p, 0, 0)),
                jax.ShapeDtypeStruct((WKV_CHUNKS, WKV_PAIRS, rows, n2), dtype))

    outs = [out(4 * c, BF16), out(2 * c, F32), out(2 * c, F32), out(2 * c, BF16), out(2 * c, BF16),
            out(2 * c, F32), out(8, F32)]
    return pl.pallas_call(
        functools.partial(_wkv_prep_kernel, reverse=reverse),
        grid=(WKV_PAIRS, WKV_CHUNKS // cb),
        in_specs=[spec] * 6,
        out_specs=[o[0] for o in outs],
        out_shape=[o[1] for o in outs],
        compiler_params=_params("parallel", "parallel"),
        name="wkv_prep_bwd" if reverse else "wkv_prep_fwd",
    )(r, lw, k, v, a, b)


def _wkv_scan_kernel(*refs):
    c, n = WKV_CHUNK, RWKV_HEAD_DIM
    ins, (yf_ref, yb_ref, s_ref) = refs[:14], refs[14:]

    @pl.when(pl.program_id(0) == 0)
    def _():
        s_ref[...] = jnp.zeros_like(s_ref)

    for d, y_ref in enumerate((yf_ref, yb_ref)):
        wr_ref, u_ref, y0_ref, mrb_ref, bh_ref, g_ref, pc_ref = ins[7 * d:7 * d + 7]
        for p in range(WKV_PAIRS):
            s0 = s_ref[d, p]
            z = _dot_t(wr_ref[0, p], s0)
            sa = z[:2 * c] + u_ref[0, p]
            y = z[2 * c:] + y0_ref[0, p] + _dot(mrb_ref[0, p], sa)
            s_ref[d, p] = s0 * pc_ref[0, p, 0:1, :] + g_ref[0, p] + _tdot(sa, bh_ref[0, p])
            y_ref[:, p * 2 * n:(p + 1) * 2 * n] = y[:c] + y[c:]


def wkv_scan(prep_f, prep_b):
    c, n2 = WKV_CHUNK, 2 * RWKV_HEAD_DIM

    def specs(chunk_of):
        return [pl.BlockSpec((1, WKV_PAIRS) + a.shape[2:], lambda s: (chunk_of(s), 0, 0, 0)) for a in prep_f]

    y_shape = jax.ShapeDtypeStruct((N_TOK, RWKV_WIDTH), F32)
    return pl.pallas_call(
        _wkv_scan_kernel,
        grid=(WKV_CHUNKS,),
        in_specs=specs(_wkv_chunk_fwd) + specs(_wkv_chunk_bwd),
        out_specs=[pl.BlockSpec((c, RWKV_WIDTH), lambda s: (_wkv_chunk_fwd(s), 0)),
                   pl.BlockSpec((c, RWKV_WIDTH), lambda s: (_wkv_chunk_bwd(s), 0))],
        out_shape=[y_shape, y_shape],
        scratch_shapes=[pltpu.VMEM((2, WKV_PAIRS, n2, n2), F32)],
        compiler_params=_params("arbitrary"),
        name="wkv_scan",
    )(*prep_f, *prep_b)


def _moe_kernel(be_ref, x_ref, w1_ref, b1_ref, w2_ref, b2_ref, g_ref, o_ref):
    del be_ref
    u = jnp.dot(x_ref[...], w1_ref[0], preferred_element_type=F32) + b1_ref[0]
    u_glu = jnp.minimum(u[:, :EXPERT_FF], SWIGLU_LIMIT)
    u_lin = jnp.clip(u[:, EXPERT_FF:], -SWIGLU_LIMIT, SWIGLU_LIMIT)
    act = u_glu * jax.nn.sigmoid(SWIGLU_ALPHA * u_glu) * (u_lin + 1.0)
    y = jnp.dot(act.astype(BF16), w2_ref[0], preferred_element_type=F32) + b2_ref[0]
    o_ref[...] = y * g_ref[...]


def moe_experts(xg, slot_gate, block_expert, w1, b1, w2, b2):
    n_slots = xg.shape[0]
    n_blocks = n_slots // DISPATCH_BLOCK
    grid_spec = pltpu.PrefetchScalarGridSpec(
        num_scalar_prefetch=1,
        grid=(n_blocks,),
        in_specs=[pl.BlockSpec((DISPATCH_BLOCK, D_MODEL), lambda i, be: (i, 0)),
                  pl.BlockSpec((1, D_MODEL, 2 * EXPERT_FF), lambda i, be: (be[i], 0, 0)),
                  pl.BlockSpec((1, 1, 2 * EXPERT_FF), lambda i, be: (be[i], 0, 0)),
                  pl.BlockSpec((1, EXPERT_FF, D_MODEL), lambda i, be: (be[i], 0, 0)),
                  pl.BlockSpec((1, 1, D_MODEL), lambda i, be: (be[i], 0, 0)),
                  pl.BlockSpec((DISPATCH_BLOCK, 1), lambda i, be: (i, 0))],
        out_specs=pl.BlockSpec((DISPATCH_BLOCK, D_MODEL), lambda i, be: (i, 0)),
    )
    return pl.pallas_call(
        _moe_kernel,
        grid_spec=grid_spec,
        out_shape=jax.ShapeDtypeStruct((n_slots, D_MODEL), F32),
        compiler_params=_params("arbitrary"),
        name="moe_experts",
    )(block_expert, xg, w1, b1.reshape(N_EXPERTS, 1, -1), w2, b2.reshape(N_EXPERTS, 1, -1),
      slot_gate.reshape(n_slots, 1))


def moe(tokens, router_w, router_b, w1, b1, w2, b2):
    n_tok = tokens.shape[0]
    n_assign = n_tok * TOP_K
    n_blocks = -(-(n_assign + N_EXPERTS * (DISPATCH_BLOCK - 1)) // DISPATCH_BLOCK)
    n_slots = n_blocks * DISPATCH_BLOCK
    logits = router_logits(tokens, router_w, router_b)
    top_val, top_idx = lax.top_k(logits, TOP_K)
    gate = jax.nn.softmax(top_val, axis=-1).reshape(-1)
    expert = top_idx.reshape(-1)
    token = jnp.repeat(jnp.arange(n_tok, dtype=jnp.int32), TOP_K)
    order = jnp.argsort(expert)
    expert_s, token_s, gate_s = expert[order], token[order], gate[order]
    counts = jnp.bincount(expert, length=N_EXPERTS)
    padded = (counts + DISPATCH_BLOCK - 1) // DISPATCH_BLOCK * DISPATCH_BLOCK
    start = jnp.cumsum(counts) - counts
    pad_end = jnp.cumsum(padded)
    pad_start = pad_end - padded
    slot = (pad_start[expert_s] + jnp.arange(n_assign) - start[expert_s]).astype(jnp.int32)
    slot_token = jnp.zeros((n_slots,), jnp.int32).at[slot].set(token_s)
    slot_gate = jnp.zeros((n_slots,), F32).at[slot].set(gate_s)
    block_expert = jnp.minimum(
        jnp.searchsorted(pad_end, jnp.arange(n_blocks) * DISPATCH_BLOCK, side='right'),
        N_EXPERTS - 1).astype(jnp.int32)
    slot_of = jnp.zeros((n_assign,), jnp.int32).at[order].set(slot).reshape(n_tok, TOP_K)
    xg = tokens.astype(BF16)[slot_token]
    y = moe_experts(xg, slot_gate, block_expert, w1.astype(BF16), b1, w2.astype(BF16), b2)
    return jnp.sum(y[slot_of], axis=1)


def _split(x, sizes):
    return jnp.split(x, np.cumsum(sizes)[:-1].tolist(), axis=-1)


def _rmsnorm(x, g):
    return x * lax.rsqrt(jnp.mean(jnp.square(x), -1, keepdims=True) + RMS_EPS) * g


def _rope_tables(rot_dim):
    n_freq = rot_dim // 4
    row = jnp.repeat(jnp.arange(SEQ // GRID_W, dtype=F32), GRID_W)
    col = (jnp.arange(SEQ) % GRID_W).astype(F32)
    freqs = ROPE_THETA ** (-jnp.arange(n_freq, dtype=F32) / n_freq)
    ang = jnp.stack([row[:, None] * freqs, col[:, None] * freqs], axis=1)
    ang = jnp.concatenate([ang, jnp.zeros((CTX_LEN, 2, n_freq), F32)], axis=0)
    return jnp.cos(ang), jnp.sin(ang)


def _rope(x, rope):
    cos, sin = rope
    t, h, dim = x.shape
    n_freq = dim // 4
    xf = x.reshape(t, h, 2, 2, n_freq)
    x1, x2 = xf[..., 0, :], xf[..., 1, :]
    cs, sn = cos[:, None], sin[:, None]
    out = jnp.stack([x1 * cs - x2 * sn, x2 * cs + x1 * sn], axis=-2)
    return out.reshape(t, h, dim)


def _shift_mix(p, mu):
    def one(z):
        edge = jnp.zeros_like(z[:1])
        prev = jnp.concatenate([edge, z[:-1]], axis=0)
        nxt = jnp.concatenate([z[1:], edge], axis=0)
        return z + mu[0] * (prev - z) + mu[1] * (nxt - z)
    return jnp.concatenate([one(p[:SEQ]), one(p[SEQ:])], axis=0)


def _in_proj_weight(w_in):
    small = w_in[:, :sum(SMALL_SPLITS)]
    rwkv = w_in[:, sum(SMALL_SPLITS):sum(SMALL_SPLITS) + RWKV_IN]
    gates = w_in[:, sum(SMALL_SPLITS) + RWKV_IN:]
    pad = lambda z, n: jnp.pad(z, ((0, 0), (0, n - z.shape[1])))
    return jnp.concatenate([gates, pad(rwkv, RWKV_IN_PAD), pad(small, SMALL_WIDTH)], axis=1).astype(BF16)


def _mixer(h, w_in, gqa_q_norm, gqa_k_norm, mla_q_norm, mla_kv_norm, mla_w_uq, mla_w_ukv,
           rwkv_mu, rwkv_w0, rwkv_w_up, rwkv_a0, rwkv_a_up, rwkv_g_up, rwkv_k_k, rwkv_k_a, rwkv_r_k,
           rwkv_gn_g, rwkv_gn_b, w_branch_up, w_o):
    t = N_TOK
    proj = matmul(h, _in_proj_weight(w_in), name="in_proj")
    gate_cols = proj[:, :N_BRANCHES * D_MODEL]
    rw = proj[:, N_BRANCHES * D_MODEL:N_BRANCHES * D_MODEL + RWKV_IN]
    small = proj[:, N_BRANCHES * D_MODEL + RWKV_IN_PAD:]
    aq, ak, av, dq, dkv, kr = _split(small[:, :sum(SMALL_SPLITS)], SMALL_SPLITS)
    rope_a = _rope_tables(GQA_HEAD_DIM)
    rope_b = _rope_tables(MLA_ROPE_DIM)

    q = _rope(_rmsnorm(aq.reshape(t, GQA_HEADS, GQA_HEAD_DIM), gqa_q_norm), rope_a) * GQA_HEAD_DIM ** -0.5
    k = _rope(_rmsnorm(ak.reshape(t, GQA_KV_HEADS, GQA_HEAD_DIM), gqa_k_norm), rope_a)
    v = av.reshape(t, GQA_KV_HEADS, GQA_HEAD_DIM)
    grp = GQA_HEADS // GQA_KV_HEADS
    q = jnp.transpose(q.reshape(t, GQA_KV_HEADS, grp, GQA_HEAD_DIM), (1, 2, 0, 3)).astype(BF16)
    k = jnp.transpose(k, (1, 0, 2)).astype(BF16)
    v = jnp.transpose(v, (1, 0, 2)).astype(BF16)
    oa = lat_ctx_attention(q, k, v, 128, "gqa")

    qb = matmul(_rmsnorm(dq, mla_q_norm).astype(BF16), mla_w_uq, name="mla_uq").reshape(t, MLA_HEADS, MLA_QK_DIM)
    kvb = matmul(_rmsnorm(dkv, mla_kv_norm).astype(BF16), mla_w_ukv, name="mla_ukv")
    kvb = kvb.reshape(t, MLA_HEADS, MLA_NOPE_DIM + MLA_V_DIM)
    scale = MLA_QK_DIM ** -0.5
    zpad = jnp.zeros((t, MLA_HEADS, MLA_CAT_DIM - MLA_QK_DIM), F32)
    q_cat = jnp.concatenate([qb[..., :MLA_NOPE_DIM] * scale, _rope(qb[..., MLA_NOPE_DIM:], rope_b) * scale, zpad], -1)
    k_rope = jnp.broadcast_to(_rope(kr[:, None, :], rope_b), (t, MLA_HEADS, MLA_ROPE_DIM))
    k_cat = jnp.concatenate([kvb[..., :MLA_NOPE_DIM], k_rope, zpad], -1)
    to_h = lambda z: jnp.transpose(z, (1, 0, 2)).astype(BF16)
    ob = lat_ctx_attention(to_h(q_cat)[:, None], to_h(k_cat), to_h(kvb[..., MLA_NOPE_DIM:]), 512, "mla")

    p = _shift_mix(rw, rwkv_mu)
    r, kx, vx, wd_f, wd_b, ad_f, ad_b, gd = _split(p, RWKV_SPLITS)
    gd_pad = jnp.pad(jax.nn.sigmoid(gd), ((0, 0), (0, 512 - RWKV_GATE_RANK)))
    g_up_pad = jnp.pad(rwkv_g_up, ((0, 512 - RWKV_GATE_RANK), (0, 0)))
    g = matmul(gd_pad.astype(BF16), g_up_pad, name="rwkv_gate")
    heads = lambda z: z.reshape(t, RWKV_HEADS, RWKV_HEAD_DIM)
    kk = heads(kx * rwkv_k_k)
    kk = (kk / jnp.maximum(jnp.linalg.norm(kk, axis=-1, keepdims=True), 1e-12)).reshape(t, RWKV_WIDTH)
    preps, k_sum = [], 0.0
    for d, (wd, ad) in enumerate(((wd_f, ad_f), (wd_b, ad_b))):
        w_lin = matmul(jnp.tanh(wd).astype(BF16), rwkv_w_up[d], bias=rwkv_w0[d], name="rwkv_decay")
        w_log = -jax.nn.softplus(-w_lin) - 0.5
        log_decay = -jnp.exp(w_log)
        a_gate = jax.nn.sigmoid(matmul(ad.astype(BF16), rwkv_a_up[d], bias=rwkv_a0[d], name="rwkv_iclr"))
        k_d = kx * (1 + (a_gate - 1) * rwkv_k_a)
        preps.append(wkv_prep(r, log_decay, k_d, vx, -kk, kk * a_gate, reverse=(d == 1)))
        k_sum = k_sum + k_d
    y_f, y_b = wkv_scan(*preps)
    y = heads(y_f + y_b)
    mu = jnp.mean(y, -1, keepdims=True)
    var = jnp.mean(jnp.square(y - mu), -1, keepdims=True)
    y = ((y - mu) * lax.rsqrt(var + GN_EPS)).reshape(t, RWKV_WIDTH) * rwkv_gn_g + rwkv_gn_b
    bonus = jnp.sum(heads(r) * heads(k_sum) * rwkv_r_k, axis=-1, keepdims=True) * heads(vx)
    oc = (y + bonus.reshape(t, RWKV_WIDTH)) * g

    gates = _split(gate_cols, (D_MODEL,) * N_BRANCHES)
    m = 0.0
    for i, (ob_i, gb) in enumerate(zip((oa, ob, oc), gates)):
        m = m + jax.nn.sigmoid(gb) * matmul(ob_i.astype(BF16), w_branch_up[i].astype(BF16), name="branch_up")
    return matmul(m.astype(BF16), w_o.astype(BF16), name="out_proj")


def kernel(x, c, ctx, c_ctx, w_ada, b_ada, w_in, gqa_q_norm, gqa_k_norm, mla_q_norm, mla_kv_norm, mla_w_uq,
           mla_w_ukv, rwkv_mu, rwkv_w0, rwkv_w_up, rwkv_a0, rwkv_a_up, rwkv_g_up, rwkv_k_k, rwkv_k_a, rwkv_r_k,
           rwkv_gn_g, rwkv_gn_b, w_branch_up, w_o, ln1_g, ln1_b, router_w, router_b, exp_w_in, exp_b_in,
           exp_w_out, exp_b_out, ln2_g, ln2_b):
    xs = jnp.concatenate([x[0], ctx[0]], axis=0)
    cond = jnp.zeros((16, D_MODEL), F32).at[0].set(jax.nn.silu(c[0])).at[1].set(jax.nn.silu(c_ctx))
    for l in range(DEPTH):
        ada = matmul(cond.astype(BF16), w_ada[l], bias=b_ada[l], tm=16, tn=512, name="adaln")
        sh1, sc1, g1, sh2, sc2, g2 = (z[:2].reshape(2, 1, D_MODEL) for z in _split(ada, (D_MODEL,) * 6))
        h = modulate(xs, sc1, sh1)
        m = _mixer(h, w_in[l], gqa_q_norm[l], gqa_k_norm[l], mla_q_norm[l], mla_kv_norm[l], mla_w_uq[l],
                   mla_w_ukv[l], rwkv_mu[l], rwkv_w0[l], rwkv_w_up[l], rwkv_a0[l], rwkv_a_up[l], rwkv_g_up[l],
                   rwkv_k_k[l], rwkv_k_a[l], rwkv_r_k[l], rwkv_gn_g[l], rwkv_gn_b[l], w_branch_up[l], w_o[l])
        xs, tokens = post_ln(xs, m, g1, sc2, sh2, ln1_g[l], ln1_b[l])
        f = moe(tokens, router_w[l], router_b[l], exp_w_in[l], exp_b_in[l], exp_w_out[l], exp_b_out[l])
        zero = jnp.zeros((2, 1, D_MODEL), F32)
        xs, _ = post_ln(xs, f, g2, zero, zero, ln2_g[l], ln2_b[l])
    return xs[:SEQ].reshape(1, SEQ, D_MODEL)
```

```python
import functools

import jax
import jax.numpy as jnp
import numpy as np
from jax import lax
from jax.experimental import pallas as pl
from jax.experimental.pallas import tpu as pltpu

F32, BF16 = jnp.float32, jnp.bfloat16

D_MODEL = 4096
SEQ = 8192
CTX_LEN = 256
N_TOK = SEQ + CTX_LEN
DEPTH = 2
GRID_W = 64
ROPE_THETA = 10000.0

GQA_HEADS, GQA_KV_HEADS, GQA_HEAD_DIM = 8, 2, 128
MLA_HEADS, MLA_Q_RANK, MLA_KV_RANK = 8, 896, 512
MLA_NOPE_DIM, MLA_ROPE_DIM, MLA_V_DIM = 128, 64, 128
MLA_QK_DIM = MLA_NOPE_DIM + MLA_ROPE_DIM
MLA_CAT_DIM = 256
RWKV_HEADS, RWKV_HEAD_DIM = 16, 64
RWKV_WIDTH = RWKV_HEADS * RWKV_HEAD_DIM
RWKV_DECAY_RANK, RWKV_ICLR_RANK, RWKV_GATE_RANK = 128, 128, 480
RWKV_SPLITS = (RWKV_WIDTH, RWKV_WIDTH, RWKV_WIDTH, RWKV_DECAY_RANK, RWKV_DECAY_RANK,
               RWKV_ICLR_RANK, RWKV_ICLR_RANK, RWKV_GATE_RANK)
RWKV_IN = sum(RWKV_SPLITS)
RWKV_IN_PAD = 4096
BRANCH_WIDTH = 1024
N_BRANCHES = 3
N_EXPERTS, TOP_K, EXPERT_FF = 32, 4, 512
SWIGLU_ALPHA, SWIGLU_LIMIT = 1.702, 7.0
DISPATCH_BLOCK = 128
DEEPNORM_ALPHA = (2 * DEPTH) ** 0.25
LN_EPS, RMS_EPS, GN_EPS = 1e-5, 1e-6, 64e-5

IN_SPLITS = (1024, 256, 256, MLA_Q_RANK, MLA_KV_RANK, MLA_ROPE_DIM, RWKV_IN, N_BRANCHES * D_MODEL)
SMALL_SPLITS = IN_SPLITS[:6]
SMALL_WIDTH = 3072
RWKV_COL0 = N_BRANCHES * D_MODEL
SMALL_COL0 = RWKV_COL0 + RWKV_IN_PAD

ROW_TILE = 256
LAT_TILES = SEQ // ROW_TILE
WKV_CHUNK = 64
WKV_PAIRS = RWKV_HEADS // 2
WKV_CHUNKS = N_TOK // WKV_CHUNK
WKV_CTX_CHUNKS = CTX_LEN // WKV_CHUNK
WKV_PREP_CHUNKS = 6
VMEM_LIMIT = 48 * 1024 * 1024
MOE_VMEM_LIMIT = 56 * 1024 * 1024


def _params(*sem, vmem=VMEM_LIMIT):
    return pltpu.CompilerParams(dimension_semantics=sem, vmem_limit_bytes=vmem)


def _dot(a, b):
    return jnp.dot(a.astype(BF16), b.astype(BF16), preferred_element_type=F32)


def _dot_t(a, b):
    return lax.dot_general(a.astype(BF16), b.astype(BF16), (((1,), (1,)), ((), ())), preferred_element_type=F32)


def _tdot(a, b):
    return lax.dot_general(a.astype(BF16), b.astype(BF16), (((0,), (0,)), ((), ())), preferred_element_type=F32)


def _mm_kernel(a_ref, b_ref, bias_ref, o_ref):
    o_ref[...] = (_dot(a_ref[...], b_ref[...]) + bias_ref[...]).astype(o_ref.dtype)


def _pick(n, prefs):
    for p in prefs:
        if n % p == 0:
            return p
    return n


def matmul(a, b, bias=None, out_dtype=F32, tm=None, tn=None, name="matmul"):
    m, kdim = a.shape
    _, n = b.shape
    tm = tm or _pick(m, (768, 512, 256, 128, 16))
    tn = tn or _pick(n, (512, 256, 128))
    if bias is None:
        bias = jnp.zeros((n,), F32)
    bias = bias.reshape(1, n).astype(F32)
    return pl.pallas_call(
        _mm_kernel,
        grid=(m // tm, n // tn),
        in_specs=[pl.BlockSpec((tm, kdim), lambda i, j: (i, 0)),
                  pl.BlockSpec((kdim, tn), lambda i, j: (0, j)),
                  pl.BlockSpec((1, tn), lambda i, j: (0, j))],
        out_specs=pl.BlockSpec((tm, tn), lambda i, j: (i, j)),
        out_shape=jax.ShapeDtypeStruct((m, n), out_dtype),
        compiler_params=_params("parallel", "arbitrary"),
        name=name,
    )(a, b, bias)


def _router_kernel(x_ref, w_ref, b_ref, o_ref):
    x = x_ref[...]
    w = w_ref[...]
    xh = x.astype(BF16)
    xl = (x - xh.astype(F32)).astype(BF16)
    wh = w.astype(BF16)
    wl = (w - wh.astype(F32)).astype(BF16)
    acc = jnp.dot(xh, wh, preferred_element_type=F32)
    acc += jnp.dot(xh, wl, preferred_element_type=F32)
    acc += jnp.dot(xl, wh, preferred_element_type=F32)
    o_ref[...] = acc + b_ref[...]


def router_logits(tokens, router_w, router_b):
    n_tok = tokens.shape[0]
    npad = 128
    w = jnp.zeros((D_MODEL, npad), F32).at[:, :N_EXPERTS].set(router_w)
    b = jnp.zeros((1, npad), F32).at[0, :N_EXPERTS].set(router_b)
    out = pl.pallas_call(
        _router_kernel,
        grid=(n_tok // ROW_TILE,),
        in_specs=[pl.BlockSpec((ROW_TILE, D_MODEL), lambda i: (i, 0)),
                  pl.BlockSpec((D_MODEL, npad), lambda i: (0, 0)),
                  pl.BlockSpec((1, npad), lambda i: (0, 0))],
        out_specs=pl.BlockSpec((ROW_TILE, npad), lambda i: (i, 0)),
        out_shape=jax.ShapeDtypeStruct((n_tok, npad), F32),
        compiler_params=_params("parallel"),
        name="router",
    )(tokens, w, b)
    return out[:, :N_EXPERTS]


def _mod_index(i):
    return (i // LAT_TILES, 0, 0)


def _modulate_kernel(x_ref, sc_ref, sh_ref, o_ref):
    o_ref[...] = (x_ref[...] * (1.0 + sc_ref[0]) + sh_ref[0]).astype(o_ref.dtype)


def modulate(x, sc, sh, out_dtype=BF16):
    n = x.shape[0]
    return pl.pallas_call(
        _modulate_kernel,
        grid=(n // ROW_TILE,),
        in_specs=[pl.BlockSpec((ROW_TILE, D_MODEL), lambda i: (i, 0)),
                  pl.BlockSpec((1, 1, D_MODEL), _mod_index),
                  pl.BlockSpec((1, 1, D_MODEL), _mod_index)],
        out_specs=pl.BlockSpec((ROW_TILE, D_MODEL), lambda i: (i, 0)),
        out_shape=jax.ShapeDtypeStruct((n, D_MODEL), out_dtype),
        compiler_params=_params("parallel"),
        name="modulate",
    )(x, sc, sh)


def _layernorm(z, g, b):
    mu = jnp.mean(z, axis=-1, keepdims=True)
    zc = z - mu
    var = jnp.mean(zc * zc, axis=-1, keepdims=True)
    return zc * lax.rsqrt(var + LN_EPS) * g + b


def _postln_mod_kernel(x_ref, m_ref, g_ref, sc_ref, sh_ref, lng_ref, lnb_ref, y_ref, t_ref, tb_ref):
    y = _layernorm(DEEPNORM_ALPHA * x_ref[...] + g_ref[0] * m_ref[...], lng_ref[...], lnb_ref[...])
    y_ref[...] = y
    t = y * (1.0 + sc_ref[0]) + sh_ref[0]
    t_ref[...] = t
    tb_ref[...] = t.astype(BF16)


def _postln_kernel(x_ref, m_ref, g_ref, lng_ref, lnb_ref, y_ref):
    y_ref[...] = _layernorm(DEEPNORM_ALPHA * x_ref[...] + g_ref[0] * m_ref[...], lng_ref[...], lnb_ref[...])


def post_ln(x, m, g, ln_g, ln_b, sc=None, sh=None):
    n = x.shape[0]
    row = pl.BlockSpec((ROW_TILE, D_MODEL), lambda i: (i, 0))
    mod = pl.BlockSpec((1, 1, D_MODEL), _mod_index)
    vec = pl.BlockSpec((1, D_MODEL), lambda i: (0, 0))
    f32_rows = jax.ShapeDtypeStruct((n, D_MODEL), F32)
    if sc is None:
        return pl.pallas_call(
            _postln_kernel,
            grid=(n // ROW_TILE,),
            in_specs=[row, row, mod, vec, vec],
            out_specs=row,
            out_shape=f32_rows,
            compiler_params=_params("parallel"),
            name="post_ln",
        )(x, m, g, ln_g.reshape(1, D_MODEL), ln_b.reshape(1, D_MODEL))
    return pl.pallas_call(
        _postln_mod_kernel,
        grid=(n // ROW_TILE,),
        in_specs=[row, row, mod, mod, mod, vec, vec],
        out_specs=[row, row, row],
        out_shape=[f32_rows, f32_rows, jax.ShapeDtypeStruct((n, D_MODEL), BF16)],
        compiler_params=_params("parallel"),
        name="post_ln_mod",
    )(x, m, g, sc, sh, ln_g.reshape(1, D_MODEL), ln_b.reshape(1, D_MODEL))


def _attn_kernel(q_ref, k_ref, v_ref, o_ref, *, nkv, tk, unroll):
    g, tq, d = q_ref.shape[1:]
    dv = v_ref.shape[-1]
    rows = g * tq
    q = q_ref[0].reshape(rows, d)

    def body(j, carry):
        m_prev, l_prev, acc = carry
        start = pl.multiple_of(j * tk, tk)
        kj = k_ref[0, pl.ds(start, tk), :]
        vj = v_ref[0, pl.ds(start, tk), :]
        s = lax.dot_general(q, kj, (((1,), (1,)), ((), ())), preferred_element_type=F32)
        m_new = jnp.maximum(m_prev, jnp.max(s, axis=-1, keepdims=True))
        alpha = jnp.exp(m_prev - m_new)
        p = jnp.exp(s - m_new)
        l_new = alpha * l_prev + jnp.sum(p, axis=-1, keepdims=True)
        acc = alpha * acc + jnp.dot(p.astype(BF16), vj, preferred_element_type=F32)
        return m_new, l_new, acc

    init = (jnp.full((rows, 1), -jnp.inf, F32), jnp.zeros((rows, 1), F32), jnp.zeros((rows, dv), F32))
    _, l, acc = lax.fori_loop(0, nkv, body, init, unroll=unroll)
    o = acc / l
    for gi in range(g):
        o_ref[:, gi * dv:(gi + 1) * dv] = o[gi * tq:(gi + 1) * tq].astype(o_ref.dtype)


def attention(q, k, v, q_rows, kv_rows, tq, tk, unroll=2, name="attn"):
    hkv, g, _, d = q.shape
    dv = v.shape[-1]
    q0, nq = q_rows
    k0, nk = kv_rows
    assert q0 % tq == 0 and nq % tq == 0 and k0 % nk == 0 and nk % tk == 0
    qb, kb = q0 // tq, k0 // nk
    return pl.pallas_call(
        functools.partial(_attn_kernel, nkv=nk // tk, tk=tk, unroll=unroll),
        grid=(hkv, nq // tq),
        in_specs=[pl.BlockSpec((1, g, tq, d), lambda h, i: (h, 0, i + qb, 0)),
                  pl.BlockSpec((1, nk, d), lambda h, i: (h, kb, 0)),
                  pl.BlockSpec((1, nk, dv), lambda h, i: (h, kb, 0))],
        out_specs=pl.BlockSpec((tq, g * dv), lambda h, i: (i, h)),
        out_shape=jax.ShapeDtypeStruct((nq, hkv * g * dv), BF16),
        compiler_params=_params("parallel", "arbitrary"),
        name=name,
    )(q, k, v)


def lat_ctx_attention(q, k, v, tq, name):
    o_l = attention(q, k, v, (0, SEQ), (0, N_TOK), tq, 384, name=name + "_lat")
    o_c = attention(q, k, v, (SEQ, CTX_LEN), (SEQ, CTX_LEN), min(tq, CTX_LEN), CTX_LEN, unroll=1,
                    name=name + "_ctx")
    return jnp.concatenate([o_l, o_c], axis=0)


def _bd(x, m_left, m_right):
    return jnp.concatenate([x * m_left, x * m_right], axis=0)


def _cumsum_f32(tri, x):
    x1 = x.astype(BF16)
    r1 = x - x1.astype(F32)
    x2 = r1.astype(BF16)
    x3 = (r1 - x2.astype(F32)).astype(BF16)
    t = tri.astype(BF16)
    return (jnp.dot(t, x1, preferred_element_type=F32) + jnp.dot(t, x2, preferred_element_type=F32)
            + jnp.dot(t, x3, preferred_element_type=F32))


def _wkv_prep_kernel(r_ref, lw_ref, k_ref, v_ref, a_ref, b_ref,
                     wr_ref, u_ref, y0_ref, mrb_ref, bh_ref, g_ref, pc_ref, *, reverse):
    c = WKV_CHUNK
    n = RWKV_HEAD_DIM
    row = lax.broadcasted_iota(jnp.int32, (c, c), 0)
    col = lax.broadcasted_iota(jnp.int32, (c, c), 1)
    tri = ((col >= row) if reverse else (col <= row)).astype(F32)
    row2 = lax.broadcasted_iota(jnp.int32, (2 * c, 2 * c), 0) % c
    col2 = lax.broadcasted_iota(jnp.int32, (2 * c, 2 * c), 1) % c
    if reverse:
        incl, strict = col2 >= row2, col2 > row2
    else:
        incl, strict = col2 <= row2, col2 < row2
    lane = lax.broadcasted_iota(jnp.int32, (1, 2 * n), 1)
    m_left = (lane < n).astype(F32)
    m_right = 1.0 - m_left
    bd = functools.partial(_bd, m_left=m_left, m_right=m_right)
    last = 0 if reverse else c - 1

    for ci in range(WKV_PREP_CHUNKS):
        rows = slice(ci * c, (ci + 1) * c)
        lw = lw_ref[rows, :]
        cum = _cumsum_f32(tri, lw)
        p_incl = jnp.exp(cum)
        p_inv = jnp.exp(-cum)
        pe = p_incl[last:last + 1, :]
        at = bd(a_ref[rows, :] * jnp.exp(cum - lw))
        rt = bd(r_ref[rows, :] * p_incl)
        bt = bd(b_ref[rows, :] * p_inv)
        kt = bd(k_ref[rows, :] * p_inv)
        v = bd(v_ref[rows, :])
        big = _dot_t(jnp.concatenate([at, rt], axis=0), jnp.concatenate([bt, kt], axis=0))
        l_ab = jnp.where(strict, big[:2 * c, :2 * c], 0.0)
        l_ak = jnp.where(strict, big[:2 * c, 2 * c:], 0.0)
        m_rb = jnp.where(incl, big[2 * c:, :2 * c], 0.0)
        m_rk = jnp.where(incl, big[2 * c:, 2 * c:], 0.0)
        lv = _dot(jnp.concatenate([l_ak, m_rk], axis=0), v)
        x = jnp.concatenate([at, lv[:2 * c]], axis=1)
        l_pow = l_ab
        x = x + _dot(l_pow, x)
        span = 1
        while span * 2 < c:
            l_pow = _dot(l_pow, l_pow)
            x = x + _dot(l_pow, x)
            span *= 2
        wr_ref[ci, 0, :2 * c, :] = x[:, :2 * n].astype(BF16)
        wr_ref[ci, 0, 2 * c:, :] = rt.astype(BF16)
        u_ref[ci, 0] = x[:, 2 * n:]
        y0_ref[ci, 0] = lv[2 * c:]
        mrb_ref[ci, 0] = m_rb.astype(BF16)
        bh_ref[ci, 0] = (bt * pe).astype(BF16)
        g_ref[ci, 0] = _tdot(v, kt * pe)
        pc_ref[ci, 0] = jnp.broadcast_to(pe, (8, 2 * n))


def _wkv_chunk_fwd(s):
    return (s + SEQ // WKV_CHUNK) % WKV_CHUNKS


def _wkv_chunk_bwd(s):
    return WKV_CHUNKS - 1 - s


def wkv_prep(r, lw, k, v, a, b, reverse):
    c, n2 = WKV_CHUNK, 2 * RWKV_HEAD_DIM
    cb = WKV_PREP_CHUNKS
    spec = pl.BlockSpec((cb * c, n2), lambda p, i: (i, p))

    def out(rows, dtype):
        return (pl.BlockSpec((cb, 1, rows, n2), lambda p, i: (i, p, 0, 0)),
                jax.ShapeDtypeStruct((WKV_CHUNKS, WKV_PAIRS, rows, n2), dtype))

    outs = [out(4 * c, BF16), out(2 * c, F32), out(2 * c, F32), out(2 * c, BF16), out(2 * c, BF16),
            out(2 * c, F32), out(8, F32)]
    return pl.pallas_call(
        functools.partial(_wkv_prep_kernel, reverse=reverse),
        grid=(WKV_PAIRS, WKV_CHUNKS // cb),
        in_specs=[spec] * 6,
        out_specs=[o[0] for o in outs],
        out_shape=[o[1] for o in outs],
        compiler_params=_params("parallel", "parallel"),
        name="wkv_prep_bwd" if reverse else "wkv_prep_fwd",
    )(r, lw, k, v, a, b)


def _wkv_scan_kernel(*refs):
    c, n = WKV_CHUNK, RWKV_HEAD_DIM
    ins, (yf_ref, yb_ref, s_ref) = refs[:14], refs[14:]

    @pl.when(pl.program_id(0) == 0)
    def _():
        s_ref[...] = jnp.zeros_like(s_ref)

    for d, y_ref in enumerate((yf_ref, yb_ref)):
        wr_ref, u_ref, y0_ref, mrb_ref, bh_ref, g_ref, pc_ref = ins[7 * d:7 * d + 7]
        for p in range(WKV_PAIRS):
            s0 = s_ref[d, p]
            z = _dot_t(wr_ref[0, p], s0)
            sa = z[:2 * c] + u_ref[0, p]
            y = z[2 * c:] + y0_ref[0, p] + _dot(mrb_ref[0, p], sa)
            s_ref[d, p] = s0 * pc_ref[0, p, 0:1, :] + g_ref[0, p] + _tdot(sa, bh_ref[0, p])
            y_ref[:, p * 2 * n:(p + 1) * 2 * n] = y[:c] + y[c:]


def wkv_scan(prep_f, prep_b):
    c, n2 = WKV_CHUNK, 2 * RWKV_HEAD_DIM

    def specs(chunk_of):
        return [pl.BlockSpec((1, WKV_PAIRS) + a.shape[2:], lambda s: (chunk_of(s), 0, 0, 0)) for a in prep_f]

    y_shape = jax.ShapeDtypeStruct((N_TOK, RWKV_WIDTH), F32)
    return pl.pallas_call(
        _wkv_scan_kernel,
        grid=(WKV_CHUNKS,),
        in_specs=specs(_wkv_chunk_fwd) + specs(_wkv_chunk_bwd),
        out_specs=[pl.BlockSpec((c, RWKV_WIDTH), lambda s: (_wkv_chunk_fwd(s), 0)),
                   pl.BlockSpec((c, RWKV_WIDTH), lambda s: (_wkv_chunk_bwd(s), 0))],
        out_shape=[y_shape, y_shape],
        scratch_shapes=[pltpu.VMEM((2, WKV_PAIRS, n2, n2), F32)],
        compiler_params=_params("arbitrary"),
        name="wkv_scan",
    )(*prep_f, *prep_b)


MOE_CAST_ROWS = 256


def _moe_kernel(be_ref, x_ref, w1_ref, b1_ref, w2_ref, b2_ref, g_ref, o_ref, w1b_ref, w2b_ref):
    i = pl.program_id(0)
    new_expert = (i == 0) | (be_ref[i] != be_ref[jnp.maximum(i - 1, 0)])

    @pl.when(new_expert)
    def _():
        def cast1(c, carry):
            rows = pl.ds(pl.multiple_of(c * MOE_CAST_ROWS, MOE_CAST_ROWS), MOE_CAST_ROWS)
            w1b_ref[rows, :] = w1_ref[0, rows, :].astype(BF16)
            return carry

        def cast2(c, carry):
            rows = pl.ds(pl.multiple_of(c * MOE_CAST_ROWS, MOE_CAST_ROWS), MOE_CAST_ROWS)
            w2b_ref[rows, :] = w2_ref[0, rows, :].astype(BF16)
            return carry

        lax.fori_loop(0, D_MODEL // MOE_CAST_ROWS, cast1, 0)
        lax.fori_loop(0, EXPERT_FF // MOE_CAST_ROWS, cast2, 0)

    u = jnp.dot(x_ref[...], w1b_ref[...], preferred_element_type=F32) + b1_ref[0]
    u_glu = jnp.minimum(u[:, :EXPERT_FF], SWIGLU_LIMIT)
    u_lin = jnp.clip(u[:, EXPERT_FF:], -SWIGLU_LIMIT, SWIGLU_LIMIT)
    act = u_glu * jax.nn.sigmoid(SWIGLU_ALPHA * u_glu) * (u_lin + 1.0)
    y = jnp.dot(act.astype(BF16), w2b_ref[...], preferred_element_type=F32) + b2_ref[0]
    o_ref[...] = y * g_ref[...]


def moe_experts(xg, slot_gate, block_expert, w1, b1, w2, b2):
    n_slots = xg.shape[0]
    n_blocks = n_slots // DISPATCH_BLOCK
    grid_spec = pltpu.PrefetchScalarGridSpec(
        num_scalar_prefetch=1,
        grid=(n_blocks,),
        in_specs=[pl.BlockSpec((DISPATCH_BLOCK, D_MODEL), lambda i, be: (i, 0)),
                  pl.BlockSpec((1, D_MODEL, 2 * EXPERT_FF), lambda i, be: (be[i], 0, 0),
                               pipeline_mode=pl.Buffered(1)),
                  pl.BlockSpec((1, 1, 2 * EXPERT_FF), lambda i, be: (be[i], 0, 0)),
                  pl.BlockSpec((1, EXPERT_FF, D_MODEL), lambda i, be: (be[i], 0, 0),
                               pipeline_mode=pl.Buffered(1)),
                  pl.BlockSpec((1, 1, D_MODEL), lambda i, be: (be[i], 0, 0)),
                  pl.BlockSpec((DISPATCH_BLOCK, 1), lambda i, be: (i, 0))],
        out_specs=pl.BlockSpec((DISPATCH_BLOCK, D_MODEL), lambda i, be: (i, 0)),
        scratch_shapes=[pltpu.VMEM((D_MODEL, 2 * EXPERT_FF), BF16), pltpu.VMEM((EXPERT_FF, D_MODEL), BF16)],
    )
    return pl.pallas_call(
        _moe_kernel,
        grid_spec=grid_spec,
        out_shape=jax.ShapeDtypeStruct((n_slots, D_MODEL), F32),
        compiler_params=_params("arbitrary", vmem=MOE_VMEM_LIMIT),
        name="moe_experts",
    )(block_expert, xg, w1, b1.reshape(N_EXPERTS, 1, -1), w2, b2.reshape(N_EXPERTS, 1, -1),
      slot_gate.reshape(n_slots, 1))


def moe(tokens, tokens_bf16, router_w, router_b, w1, b1, w2, b2):
    n_tok = tokens.shape[0]
    n_assign = n_tok * TOP_K
    n_blocks = -(-(n_assign + N_EXPERTS * (DISPATCH_BLOCK - 1)) // DISPATCH_BLOCK)
    n_slots = n_blocks * DISPATCH_BLOCK
    logits = router_logits(tokens, router_w, router_b)
    top_val, top_idx = lax.top_k(logits, TOP_K)
    gate = jax.nn.softmax(top_val, axis=-1).reshape(-1)
    expert = top_idx.reshape(-1)
    token = jnp.repeat(jnp.arange(n_tok, dtype=jnp.int32), TOP_K)
    order = jnp.argsort(expert)
    expert_s, token_s, gate_s = expert[order], token[order], gate[order]
    counts = jnp.bincount(expert, length=N_EXPERTS)
    padded = (counts + DISPATCH_BLOCK - 1) // DISPATCH_BLOCK * DISPATCH_BLOCK
    start = jnp.cumsum(counts) - counts
    pad_end = jnp.cumsum(padded)
    pad_start = pad_end - padded
    slot = (pad_start[expert_s] + jnp.arange(n_assign) - start[expert_s]).astype(jnp.int32)
    slot_token = jnp.zeros((n_slots,), jnp.int32).at[slot].set(token_s)
    slot_gate = jnp.zeros((n_slots,), F32).at[slot].set(gate_s)
    block_expert = jnp.minimum(
        jnp.searchsorted(pad_end, jnp.arange(n_blocks) * DISPATCH_BLOCK, side='right'),
        N_EXPERTS - 1).astype(jnp.int32)
    slot_of = jnp.zeros((n_assign,), jnp.int32).at[order].set(slot).reshape(n_tok, TOP_K)
    xg = tokens_bf16[slot_token]
    y = moe_experts(xg, slot_gate, block_expert, w1, b1, w2, b2)
    return jnp.sum(y[slot_of], axis=1)


def _rope_lanes(x, cos, sin, half):
    lane = lax.broadcasted_iota(jnp.int32, x.shape, 1)
    up = pltpu.roll(x, 128 - half, 1)
    down = pltpu.roll(x, half, 1)
    return x * cos + jnp.where(lane % (2 * half) < half, up, down) * sin


def _rms(x, g):
    return x * lax.rsqrt(jnp.mean(x * x, axis=-1, keepdims=True) + RMS_EPS) * g


def _attn_prep_kernel(pa_ref, pb_ref, pc_ref, gq_ref, gk_ref, gmq_ref, gmkv_ref, cosa_ref, sina_ref,
                      cosb_ref, sinb_ref, wuq_ref, wukv_ref, qg_ref, kg_ref, vg_ref, qc_ref, kc_ref, vm_ref):
    hd = GQA_HEAD_DIM
    ca, sa, cb, sb = cosa_ref[...], sina_ref[...], cosb_ref[...], sinb_ref[...]
    grp = GQA_HEADS // GQA_KV_HEADS
    for h in range(GQA_HEADS):
        q = _rope_lanes(_rms(pa_ref[:, h * hd:(h + 1) * hd], gq_ref[...]), ca, sa, hd // 4)
        qg_ref[h // grp, h % grp] = (q * GQA_HEAD_DIM ** -0.5).astype(BF16)
    for h in range(GQA_KV_HEADS):
        k = _rope_lanes(_rms(pb_ref[:, h * hd:(h + 1) * hd], gk_ref[...]), ca, sa, hd // 4)
        kg_ref[h] = k.astype(BF16)
        vg_ref[h] = pb_ref[:, (GQA_KV_HEADS + h) * hd:(GQA_KV_HEADS + h + 1) * hd].astype(BF16)
    dkvn = _rms(pb_ref[:, 2 * GQA_KV_HEADS * hd:], gmkv_ref[...])
    dqn = _rms(pc_ref[:, :MLA_Q_RANK], gmq_ref[...])
    k_rope = _rope_lanes(pc_ref[:, MLA_Q_RANK:], cb, sb, MLA_ROPE_DIM // 4)
    qb = _dot(dqn, wuq_ref[...])
    kvb = _dot(dkvn, wukv_ref[...])
    scale = MLA_QK_DIM ** -0.5
    for h in range(MLA_HEADS):
        base = h * MLA_CAT_DIM
        q_nope = qb[:, base:base + 128]
        q_rope = _rope_lanes(qb[:, base + 128:base + 256], cb, sb, MLA_ROPE_DIM // 4)
        qc_ref[h, 0] = (jnp.concatenate([q_nope, q_rope], axis=1) * scale).astype(BF16)
        kc_ref[h] = jnp.concatenate([kvb[:, base:base + 128], k_rope], axis=1).astype(BF16)
        vm_ref[h] = kvb[:, base + 128:base + 256].astype(BF16)


def _rope_lane_tables(rot_dim):
    n_freq = rot_dim // 4
    row = jnp.repeat(jnp.arange(SEQ // GRID_W, dtype=F32), GRID_W)
    col = (jnp.arange(SEQ) % GRID_W).astype(F32)
    freqs = ROPE_THETA ** (-jnp.arange(n_freq, dtype=F32) / n_freq)
    ar, ac = row[:, None] * freqs, col[:, None] * freqs
    cos = jnp.concatenate([jnp.cos(ar), jnp.cos(ar), jnp.cos(ac), jnp.cos(ac)], axis=1)
    sin = jnp.concatenate([-jnp.sin(ar), jnp.sin(ar), -jnp.sin(ac), jnp.sin(ac)], axis=1)
    cos = jnp.pad(cos, ((0, CTX_LEN), (0, 128 - rot_dim)), constant_values=1.0)
    sin = jnp.pad(sin, ((0, CTX_LEN), (0, 128 - rot_dim)))
    return cos, sin


def attn_prep(proj, gqa_q_norm, gqa_k_norm, mla_q_norm, mla_kv_norm, mla_w_uq, mla_w_ukv):
    t = N_TOK
    cos_a, sin_a = _rope_lane_tables(GQA_HEAD_DIM)
    cos_b, sin_b = _rope_lane_tables(MLA_ROPE_DIM)
    w_uq = jnp.pad(mla_w_uq.reshape(MLA_Q_RANK, MLA_HEADS, MLA_QK_DIM),
                   ((0, 0), (0, 0), (0, MLA_CAT_DIM - MLA_QK_DIM))).reshape(MLA_Q_RANK, -1).astype(BF16)
    w_ukv = mla_w_ukv.astype(BF16)
    first = SMALL_COL0 // 1024
    col = lambda j: pl.BlockSpec((ROW_TILE, 1024), lambda i: (i, first + j))
    vec = lambda n: pl.BlockSpec((1, n), lambda i: (0, 0))
    tab = pl.BlockSpec((ROW_TILE, 128), lambda i: (i, 0))
    full = lambda a: pl.BlockSpec(a.shape, lambda i: (0, 0))
    grp = GQA_HEADS // GQA_KV_HEADS
    hd = GQA_HEAD_DIM
    outs = [
        (pl.BlockSpec((GQA_KV_HEADS, grp, ROW_TILE, hd), lambda i: (0, 0, i, 0)), (GQA_KV_HEADS, grp, t, hd)),
        (pl.BlockSpec((GQA_KV_HEADS, ROW_TILE, hd), lambda i: (0, i, 0)), (GQA_KV_HEADS, t, hd)),
        (pl.BlockSpec((GQA_KV_HEADS, ROW_TILE, hd), lambda i: (0, i, 0)), (GQA_KV_HEADS, t, hd)),
        (pl.BlockSpec((MLA_HEADS, 1, ROW_TILE, MLA_CAT_DIM), lambda i: (0, 0, i, 0)), (MLA_HEADS, 1, t, MLA_CAT_DIM)),
        (pl.BlockSpec((MLA_HEADS, ROW_TILE, MLA_CAT_DIM), lambda i: (0, i, 0)), (MLA_HEADS, t, MLA_CAT_DIM)),
        (pl.BlockSpec((MLA_HEADS, ROW_TILE, MLA_V_DIM), lambda i: (0, i, 0)), (MLA_HEADS, t, MLA_V_DIM)),
    ]
    return pl.pallas_call(
        _attn_prep_kernel,
        grid=(t // ROW_TILE,),
        in_specs=[col(0), col(1), col(2), vec(hd), vec(hd), vec(MLA_Q_RANK), vec(MLA_KV_RANK),
                  tab, tab, tab, tab, full(w_uq), full(w_ukv)],
        out_specs=[o[0] for o in outs],
        out_shape=[jax.ShapeDtypeStruct(o[1], BF16) for o in outs],
        compiler_params=_params("parallel"),
        name="attn_prep",
    )(proj, proj, proj, gqa_q_norm.reshape(1, -1), gqa_k_norm.reshape(1, -1), mla_q_norm.reshape(1, -1),
      mla_kv_norm.reshape(1, -1), cos_a, sin_a, cos_b, sin_b, w_uq, w_ukv)


def _group_sum(x, ones_bd):
    hi = x.astype(BF16)
    lo = (x - hi.astype(F32)).astype(BF16)
    return jnp.dot(hi, ones_bd, preferred_element_type=F32) + jnp.dot(lo, ones_bd, preferred_element_type=F32)


def _softplus(z):
    return jnp.maximum(z, 0.0) + jnp.log(1.0 + jnp.exp(-jnp.abs(z)))


def _rwkv_prep_kernel(p_ref, prev_ref, next_ref, mu_ref, wup_ref, w0_ref, aup_ref, a0_ref, gup_ref,
                      kk_ref, ka_ref, rk_ref, ones_ref,
                      r_ref, v_ref, a_ref, lwf_ref, kdf_ref, bf_ref, lwb_ref, kdb_ref, bb_ref, g_ref, bonus_ref):
    i = pl.program_id(0)
    w = RWKV_WIDTH
    p = p_ref[...]
    n = p.shape[0]
    rowid = lax.broadcasted_iota(jnp.int32, (n, 1), 0)
    has_prev = jnp.where(i % LAT_TILES == 0, 0.0, 1.0)
    has_next = jnp.where((i == LAT_TILES - 1) | (i == LAT_TILES), 0.0, 1.0)
    prev = jnp.where(rowid == 0, prev_ref[7:8, :] * has_prev, pltpu.roll(p, 1, 0))
    nxt = jnp.where(rowid == n - 1, next_ref[0:1, :] * has_next, pltpu.roll(p, n - 1, 0))
    p = p + mu_ref[0:1, :] * (prev - p) + mu_ref[1:2, :] * (nxt - p)

    ones_bd = ones_ref[...]
    r, kx, vx = p[:, :w], p[:, w:2 * w], p[:, 2 * w:3 * w]
    low = p[:, 3 * w:]
    g_ref[...] = _dot(jax.nn.sigmoid(low[:, 512:]), gup_ref[...])
    kk = kx * kk_ref[...]
    kk = kk / jnp.maximum(jnp.sqrt(_group_sum(kk * kk, ones_bd)), 1e-12)
    r_ref[...] = r
    v_ref[...] = vx
    a_ref[...] = -kk
    k_sum = jnp.zeros_like(kx)
    for d, (lw_ref, kd_ref, b_ref) in enumerate(((lwf_ref, kdf_ref, bf_ref), (lwb_ref, kdb_ref, bb_ref))):
        wd = low[:, d * 128:(d + 1) * 128]
        ad = low[:, 256 + d * 128:256 + (d + 1) * 128]
        w_lin = _dot(jnp.tanh(wd), wup_ref[d]) + w0_ref[d:d + 1, :]
        lw_ref[...] = -jnp.exp(-_softplus(-w_lin) - 0.5)
        a_gate = jax.nn.sigmoid(_dot(ad, aup_ref[d]) + a0_ref[d:d + 1, :])
        k_d = kx * (1.0 + (a_gate - 1.0) * ka_ref[...])
        kd_ref[...] = k_d
        b_ref[...] = kk * a_gate
        k_sum = k_sum + k_d
    bonus_ref[...] = _group_sum(r * k_sum * rk_ref[...], ones_bd) * vx


def _head_ones():
    h = jnp.arange(RWKV_WIDTH) // RWKV_HEAD_DIM
    return (h[:, None] == h[None, :]).astype(BF16)


def rwkv_prep(proj, rwkv_mu, rwkv_w0, rwkv_w_up, rwkv_a0, rwkv_a_up, rwkv_g_up, rwkv_k_k, rwkv_k_a, rwkv_r_k):
    t, w = N_TOK, RWKV_WIDTH
    cb = RWKV_COL0 // RWKV_IN_PAD
    n8 = t // 8
    per8 = ROW_TILE // 8
    mu = jnp.pad(rwkv_mu, ((0, 6), (0, RWKV_IN_PAD - RWKV_IN)))
    g_up = jnp.pad(rwkv_g_up, ((0, 512 - RWKV_GATE_RANK), (0, 0))).astype(BF16)
    full = lambda a: pl.BlockSpec(a.shape, lambda i: (0,) * a.ndim)
    consts = [mu, rwkv_w_up.astype(BF16), rwkv_w0, rwkv_a_up.astype(BF16), rwkv_a0, g_up,
              rwkv_k_k.reshape(1, w), rwkv_k_a.reshape(1, w), rwkv_r_k.reshape(1, w), _head_ones()]
    out_spec = pl.BlockSpec((ROW_TILE, w), lambda i: (i, 0))
    return pl.pallas_call(
        _rwkv_prep_kernel,
        grid=(t // ROW_TILE,),
        in_specs=[pl.BlockSpec((ROW_TILE, RWKV_IN_PAD), lambda i: (i, cb)),
                  pl.BlockSpec((8, RWKV_IN_PAD), lambda i: (jnp.maximum(i * per8 - 1, 0), cb)),
                  pl.BlockSpec((8, RWKV_IN_PAD), lambda i: (jnp.minimum((i + 1) * per8, n8 - 1), cb))]
                 + [full(a) for a in consts],
        out_specs=[out_spec] * 11,
        out_shape=[jax.ShapeDtypeStruct((t, w), F32)] * 11,
        compiler_params=_params("parallel"),
        name="rwkv_prep",
    )(proj, proj, proj, *consts)


def _rwkv_out_kernel(yf_ref, yb_ref, bonus_ref, g_ref, gng_ref, gnb_ref, ones_ref, o_ref):
    ones_bd = ones_ref[...]
    y = yf_ref[...] + yb_ref[...]
    yc = y - _group_sum(y, ones_bd) * (1.0 / RWKV_HEAD_DIM)
    var = _group_sum(yc * yc, ones_bd) * (1.0 / RWKV_HEAD_DIM)
    yn = yc * lax.rsqrt(var + GN_EPS) * gng_ref[...] + gnb_ref[...]
    o_ref[...] = ((yn + bonus_ref[...]) * g_ref[...]).astype(o_ref.dtype)


def rwkv_out(y_f, y_b, bonus, g, gn_g, gn_b):
    t, w = N_TOK, RWKV_WIDTH
    row = pl.BlockSpec((ROW_TILE, w), lambda i: (i, 0))
    vec = pl.BlockSpec((1, w), lambda i: (0, 0))
    ones = _head_ones()
    return pl.pallas_call(
        _rwkv_out_kernel,
        grid=(t // ROW_TILE,),
        in_specs=[row, row, row, row, vec, vec, pl.BlockSpec(ones.shape, lambda i: (0, 0))],
        out_specs=row,
        out_shape=jax.ShapeDtypeStruct((t, w), BF16),
        compiler_params=_params("parallel"),
        name="rwkv_out",
    )(y_f, y_b, bonus, g, gn_g.reshape(1, w), gn_b.reshape(1, w), ones)


def _merge_kernel(oa_ref, ob_ref, oc_ref, wa_ref, wb_ref, wc_ref, ga_ref, gb_ref, gc_ref, m_ref):
    acc = jax.nn.sigmoid(ga_ref[...]) * _dot(oa_ref[...], wa_ref[0])
    acc += jax.nn.sigmoid(gb_ref[...]) * _dot(ob_ref[...], wb_ref[0])
    acc += jax.nn.sigmoid(gc_ref[...]) * _dot(oc_ref[...], wc_ref[0])
    m_ref[...] = acc.astype(m_ref.dtype)


def merge(oa, ob, oc, proj, w_branch_up):
    t = N_TOK
    tm, tn = 768, 512
    nj = D_MODEL // tn
    wb = w_branch_up.astype(BF16)
    o_spec = pl.BlockSpec((tm, BRANCH_WIDTH), lambda i, j: (i, 0))
    w_spec = lambda b: pl.BlockSpec((1, BRANCH_WIDTH, tn), lambda i, j: (b, 0, j))
    g_spec = lambda b: pl.BlockSpec((tm, tn), lambda i, j: (i, b * nj + j))
    return pl.pallas_call(
        _merge_kernel,
        grid=(t // tm, nj),
        in_specs=[o_spec, o_spec, o_spec, w_spec(0), w_spec(1), w_spec(2), g_spec(0), g_spec(1), g_spec(2)],
        out_specs=pl.BlockSpec((tm, tn), lambda i, j: (i, j)),
        out_shape=jax.ShapeDtypeStruct((t, D_MODEL), BF16),
        compiler_params=_params("parallel", "arbitrary"),
        name="merge",
    )(oa, ob, oc, wb, wb, wb, proj, proj, proj)


def _split(x, sizes):
    return jnp.split(x, np.cumsum(sizes)[:-1].tolist(), axis=-1)


def _in_proj_weight(w_in):
    aq, ak, av, dq, dkv, kr, rwkv, gates = _split(w_in, IN_SPLITS)
    small = jnp.concatenate([aq, ak, av, dkv, dq, kr], axis=1)
    pad = lambda z, n: jnp.pad(z, ((0, 0), (0, n - z.shape[1])))
    return jnp.concatenate([gates, pad(rwkv, RWKV_IN_PAD), pad(small, SMALL_WIDTH)], axis=1).astype(BF16)


def _mixer(h, w_in, gqa_q_norm, gqa_k_norm, mla_q_norm, mla_kv_norm, mla_w_uq, mla_w_ukv,
           rwkv_mu, rwkv_w0, rwkv_w_up, rwkv_a0, rwkv_a_up, rwkv_g_up, rwkv_k_k, rwkv_k_a, rwkv_r_k,
           rwkv_gn_g, rwkv_gn_b, w_branch_up, w_o):
    proj = matmul(h, _in_proj_weight(w_in), name="in_proj")

    qg, kg, vg, qc, kc, vm = attn_prep(proj, gqa_q_norm, gqa_k_norm, mla_q_norm, mla_kv_norm, mla_w_uq, mla_w_ukv)
    oa = lat_ctx_attention(qg, kg, vg, 128, "gqa")
    ob = lat_ctx_attention(qc, kc, vm, 512, "mla")

    r, v, a, lw_f, kd_f, b_f, lw_b, kd_b, b_b, g, bonus = rwkv_prep(
        proj, rwkv_mu, rwkv_w0, rwkv_w_up, rwkv_a0, rwkv_a_up, rwkv_g_up, rwkv_k_k, rwkv_k_a, rwkv_r_k)
    y_f, y_b = wkv_scan(wkv_prep(r, lw_f, kd_f, v, a, b_f, reverse=False),
                        wkv_prep(r, lw_b, kd_b, v, a, b_b, reverse=True))
    oc = rwkv_out(y_f, y_b, bonus, g, rwkv_gn_g, rwkv_gn_b)

    m = merge(oa, ob, oc, proj, w_branch_up)
    return matmul(m, w_o.astype(BF16), name="out_proj")


def kernel(x, c, ctx, c_ctx, w_ada, b_ada, w_in, gqa_q_norm, gqa_k_norm, mla_q_norm, mla_kv_norm, mla_w_uq,
           mla_w_ukv, rwkv_mu, rwkv_w0, rwkv_w_up, rwkv_a0, rwkv_a_up, rwkv_g_up, rwkv_k_k, rwkv_k_a, rwkv_r_k,
           rwkv_gn_g, rwkv_gn_b, w_branch_up, w_o, ln1_g, ln1_b, router_w, router_b, exp_w_in, exp_b_in,
           exp_w_out, exp_b_out, ln2_g, ln2_b):
    xs = jnp.concatenate([x[0], ctx[0]], axis=0)
    cond = jnp.zeros((16, D_MODEL), F32).at[0].set(jax.nn.silu(c[0])).at[1].set(jax.nn.silu(c_ctx))
    for l in range(DEPTH):
        ada = matmul(cond.astype(BF16), w_ada[l], bias=b_ada[l], tm=16, tn=512, name="adaln")
        sh1, sc1, g1, sh2, sc2, g2 = (z[:2].reshape(2, 1, D_MODEL) for z in _split(ada, (D_MODEL,) * 6))
        h = modulate(xs, sc1, sh1)
        m = _mixer(h, w_in[l], gqa_q_norm[l], gqa_k_norm[l], mla_q_norm[l], mla_kv_norm[l], mla_w_uq[l],
                   mla_w_ukv[l], rwkv_mu[l], rwkv_w0[l], rwkv_w_up[l], rwkv_a0[l], rwkv_a_up[l], rwkv_g_up[l],
                   rwkv_k_k[l], rwkv_k_a[l], rwkv_r_k[l], rwkv_gn_g[l], rwkv_gn_b[l], w_branch_up[l], w_o[l])
        xs, tokens, tokens_bf16 = post_ln(xs, m, g1, ln1_g[l], ln1_b[l], sc2, sh2)
        f = moe(tokens, tokens_bf16, router_w[l], router_b[l], exp_w_in[l], exp_b_in[l], exp_w_out[l],
                exp_b_out[l])
        xs = post_ln(xs, f, g2, ln2_g[l], ln2_b[l])
    return xs[:SEQ].reshape(1, SEQ, D_MODEL)
```

```python
import functools

import jax
import jax.numpy as jnp
import numpy as np
from jax import lax
from jax.experimental import pallas as pl
from jax.experimental.pallas import tpu as pltpu

F32, BF16 = jnp.float32, jnp.bfloat16

D_MODEL = 4096
SEQ = 8192
CTX_LEN = 256
N_TOK = SEQ + CTX_LEN
DEPTH = 2
GRID_W = 64
ROPE_THETA = 10000.0

GQA_HEADS, GQA_KV_HEADS, GQA_HEAD_DIM = 8, 2, 128
MLA_HEADS, MLA_Q_RANK, MLA_KV_RANK = 8, 896, 512
MLA_NOPE_DIM, MLA_ROPE_DIM, MLA_V_DIM = 128, 64, 128
MLA_QK_DIM = MLA_NOPE_DIM + MLA_ROPE_DIM
MLA_CAT_DIM = 256
RWKV_HEADS, RWKV_HEAD_DIM = 16, 64
RWKV_WIDTH = RWKV_HEADS * RWKV_HEAD_DIM
RWKV_DECAY_RANK, RWKV_ICLR_RANK, RWKV_GATE_RANK = 128, 128, 480
RWKV_SPLITS = (RWKV_WIDTH, RWKV_WIDTH, RWKV_WIDTH, RWKV_DECAY_RANK, RWKV_DECAY_RANK,
               RWKV_ICLR_RANK, RWKV_ICLR_RANK, RWKV_GATE_RANK)
RWKV_IN = sum(RWKV_SPLITS)
RWKV_IN_PAD = 4096
BRANCH_WIDTH = 1024
N_BRANCHES = 3
N_EXPERTS, TOP_K, EXPERT_FF = 32, 4, 512
SWIGLU_ALPHA, SWIGLU_LIMIT = 1.702, 7.0
DISPATCH_BLOCK = 128
DEEPNORM_ALPHA = (2 * DEPTH) ** 0.25
LN_EPS, RMS_EPS, GN_EPS = 1e-5, 1e-6, 64e-5

IN_SPLITS = (1024, 256, 256, MLA_Q_RANK, MLA_KV_RANK, MLA_ROPE_DIM, RWKV_IN, N_BRANCHES * D_MODEL)
SMALL_SPLITS = IN_SPLITS[:6]
SMALL_WIDTH = 3072
RWKV_COL0 = N_BRANCHES * D_MODEL
SMALL_COL0 = RWKV_COL0 + RWKV_IN_PAD

ROW_TILE = 256
LAT_TILES = SEQ // ROW_TILE
WKV_CHUNK = 64
WKV_PAIRS = RWKV_HEADS // 2
WKV_CHUNKS = N_TOK // WKV_CHUNK
WKV_CTX_CHUNKS = CTX_LEN // WKV_CHUNK
WKV_PREP_CHUNKS = 12
ATTN_KV_TILE = 768
VMEM_LIMIT = 48 * 1024 * 1024
MOE_VMEM_LIMIT = 56 * 1024 * 1024


def _params(*sem, vmem=VMEM_LIMIT):
    return pltpu.CompilerParams(dimension_semantics=sem, vmem_limit_bytes=vmem)


def _dot(a, b):
    return jnp.dot(a.astype(BF16), b.astype(BF16), preferred_element_type=F32)


def _dot_t(a, b):
    return lax.dot_general(a.astype(BF16), b.astype(BF16), (((1,), (1,)), ((), ())), preferred_element_type=F32)


def _tdot(a, b):
    return lax.dot_general(a.astype(BF16), b.astype(BF16), (((0,), (0,)), ((), ())), preferred_element_type=F32)


def _mm_kernel(a_ref, b_ref, bias_ref, o_ref):
    o_ref[...] = (_dot(a_ref[...], b_ref[...]) + bias_ref[...]).astype(o_ref.dtype)


def _pick(n, prefs):
    for p in prefs:
        if n % p == 0:
            return p
    return n


def matmul(a, b, bias=None, out_dtype=F32, tm=None, tn=None, layer=None, name="matmul"):
    m, kdim = a.shape
    n = b.shape[-1]
    tm = tm or _pick(m, (768, 512, 256, 128, 16))
    tn = tn or _pick(n, (512, 256, 128))
    if bias is None:
        bias = jnp.zeros((n,), F32)
    bias = bias.reshape(1, n).astype(F32)
    if layer is None:
        b_spec = pl.BlockSpec((kdim, tn), lambda i, j: (0, j))
    else:
        b_spec = pl.BlockSpec((None, kdim, tn), lambda i, j: (layer, 0, j))
    return pl.pallas_call(
        _mm_kernel,
        grid=(m // tm, n // tn),
        in_specs=[pl.BlockSpec((tm, kdim), lambda i, j: (i, 0)),
                  b_spec,
                  pl.BlockSpec((1, tn), lambda i, j: (0, j))],
        out_specs=pl.BlockSpec((tm, tn), lambda i, j: (i, j)),
        out_shape=jax.ShapeDtypeStruct((m, n), out_dtype),
        compiler_params=_params("parallel", "arbitrary"),
        name=name,
    )(a, b, bias)


def _router_kernel(x_ref, w_ref, b_ref, o_ref):
    x = x_ref[...]
    w = w_ref[...]
    xh = x.astype(BF16)
    xl = (x - xh.astype(F32)).astype(BF16)
    wh = w.astype(BF16)
    wl = (w - wh.astype(F32)).astype(BF16)
    acc = jnp.dot(xh, wh, preferred_element_type=F32)
    acc += jnp.dot(xh, wl, preferred_element_type=F32)
    acc += jnp.dot(xl, wh, preferred_element_type=F32)
    o_ref[...] = acc + b_ref[...]


def router_logits(tokens, router_w, router_b):
    n_tok = tokens.shape[0]
    npad = 128
    w = jnp.zeros((D_MODEL, npad), F32).at[:, :N_EXPERTS].set(router_w)
    b = jnp.zeros((1, npad), F32).at[0, :N_EXPERTS].set(router_b)
    out = pl.pallas_call(
        _router_kernel,
        grid=(n_tok // ROW_TILE,),
        in_specs=[pl.BlockSpec((ROW_TILE, D_MODEL), lambda i: (i, 0)),
                  pl.BlockSpec((D_MODEL, npad), lambda i: (0, 0)),
                  pl.BlockSpec((1, npad), lambda i: (0, 0))],
        out_specs=pl.BlockSpec((ROW_TILE, npad), lambda i: (i, 0)),
        out_shape=jax.ShapeDtypeStruct((n_tok, npad), F32),
        compiler_params=_params("parallel"),
        name="router",
    )(tokens, w, b)
    return out[:, :N_EXPERTS]


def _mod_index(i):
    return (i // LAT_TILES, 0, 0)


def _modulate_kernel(x_ref, sc_ref, sh_ref, o_ref):
    o_ref[...] = (x_ref[...] * (1.0 + sc_ref[0]) + sh_ref[0]).astype(o_ref.dtype)


def modulate(x, sc, sh, out_dtype=BF16):
    n = x.shape[0]
    return pl.pallas_call(
        _modulate_kernel,
        grid=(n // ROW_TILE,),
        in_specs=[pl.BlockSpec((ROW_TILE, D_MODEL), lambda i: (i, 0)),
                  pl.BlockSpec((1, 1, D_MODEL), _mod_index),
                  pl.BlockSpec((1, 1, D_MODEL), _mod_index)],
        out_specs=pl.BlockSpec((ROW_TILE, D_MODEL), lambda i: (i, 0)),
        out_shape=jax.ShapeDtypeStruct((n, D_MODEL), out_dtype),
        compiler_params=_params("parallel"),
        name="modulate",
    )(x, sc, sh)


def _layernorm(z, g, b):
    mu = jnp.mean(z, axis=-1, keepdims=True)
    zc = z - mu
    var = jnp.mean(zc * zc, axis=-1, keepdims=True)
    return zc * lax.rsqrt(var + LN_EPS) * g + b


def _postln_mod_kernel(x_ref, m_ref, g_ref, sc_ref, sh_ref, lng_ref, lnb_ref, y_ref, t_ref, tb_ref):
    y = _layernorm(DEEPNORM_ALPHA * x_ref[...] + g_ref[0] * m_ref[...], lng_ref[...], lnb_ref[...])
    y_ref[...] = y
    t = y * (1.0 + sc_ref[0]) + sh_ref[0]
    t_ref[...] = t
    tb_ref[...] = t.astype(BF16)


def _postln_kernel(x_ref, m_ref, g_ref, lng_ref, lnb_ref, y_ref):
    y_ref[...] = _layernorm(DEEPNORM_ALPHA * x_ref[...] + g_ref[0] * m_ref[...], lng_ref[...], lnb_ref[...])


def post_ln(x, m, g, ln_g, ln_b, sc=None, sh=None):
    n = x.shape[0]
    row = pl.BlockSpec((ROW_TILE, D_MODEL), lambda i: (i, 0))
    mod = pl.BlockSpec((1, 1, D_MODEL), _mod_index)
    vec = pl.BlockSpec((1, D_MODEL), lambda i: (0, 0))
    f32_rows = jax.ShapeDtypeStruct((n, D_MODEL), F32)
    if sc is None:
        return pl.pallas_call(
            _postln_kernel,
            grid=(n // ROW_TILE,),
            in_specs=[row, row, mod, vec, vec],
            out_specs=row,
            out_shape=f32_rows,
            compiler_params=_params("parallel"),
            name="post_ln",
        )(x, m, g, ln_g.reshape(1, D_MODEL), ln_b.reshape(1, D_MODEL))
    return pl.pallas_call(
        _postln_mod_kernel,
        grid=(n // ROW_TILE,),
        in_specs=[row, row, mod, mod, mod, vec, vec],
        out_specs=[row, row, row],
        out_shape=[f32_rows, f32_rows, jax.ShapeDtypeStruct((n, D_MODEL), BF16)],
        compiler_params=_params("parallel"),
        name="post_ln_mod",
    )(x, m, g, sc, sh, ln_g.reshape(1, D_MODEL), ln_b.reshape(1, D_MODEL))


def _attn_kernel(q_ref, k_ref, v_ref, o_ref, *, nkv, tk, unroll):
    g, tq, d = q_ref.shape[1:]
    dv = v_ref.shape[-1]
    rows = g * tq
    q = q_ref[0].reshape(rows, d)

    def body(j, carry):
        m_prev, l_prev, acc = carry
        start = pl.multiple_of(j * tk, tk)
        kj = k_ref[0, pl.ds(start, tk), :]
        vj = v_ref[0, pl.ds(start, tk), :]
        s = lax.dot_general(q, kj, (((1,), (1,)), ((), ())), preferred_element_type=F32)
        m_new = jnp.maximum(m_prev, jnp.max(s, axis=-1, keepdims=True))
        alpha = jnp.exp(m_prev - m_new)
        p = jnp.exp(s - m_new)
        l_new = alpha * l_prev + jnp.sum(p, axis=-1, keepdims=True)
        acc = alpha * acc + jnp.dot(p.astype(BF16), vj, preferred_element_type=F32)
        return m_new, l_new, acc

    init = (jnp.full((rows, 1), -jnp.inf, F32), jnp.zeros((rows, 1), F32), jnp.zeros((rows, dv), F32))
    _, l, acc = lax.fori_loop(0, nkv, body, init, unroll=unroll)
    o = acc / l
    for gi in range(g):
        o_ref[:, gi * dv:(gi + 1) * dv] = o[gi * tq:(gi + 1) * tq].astype(o_ref.dtype)


def attention(q, k, v, q_rows, kv_rows, tq, tk, unroll=2, name="attn"):
    hkv, g, _, d = q.shape
    dv = v.shape[-1]
    q0, nq = q_rows
    k0, nk = kv_rows
    assert q0 % tq == 0 and nq % tq == 0 and k0 % nk == 0 and nk % tk == 0
    qb, kb = q0 // tq, k0 // nk
    return pl.pallas_call(
        functools.partial(_attn_kernel, nkv=nk // tk, tk=tk, unroll=unroll),
        grid=(hkv, nq // tq),
        in_specs=[pl.BlockSpec((1, g, tq, d), lambda h, i: (h, 0, i + qb, 0)),
                  pl.BlockSpec((1, nk, d), lambda h, i: (h, kb, 0)),
                  pl.BlockSpec((1, nk, dv), lambda h, i: (h, kb, 0))],
        out_specs=pl.BlockSpec((tq, g * dv), lambda h, i: (i, h)),
        out_shape=jax.ShapeDtypeStruct((nq, hkv * g * dv), BF16),
        compiler_params=_params("parallel", "arbitrary"),
        name=name,
    )(q, k, v)


def lat_ctx_attention(q, k, v, tq, name):
    o_l = attention(q, k, v, (0, SEQ), (0, N_TOK), tq, ATTN_KV_TILE, unroll=N_TOK // ATTN_KV_TILE,
                    name=name + "_lat")
    o_c = attention(q, k, v, (SEQ, CTX_LEN), (SEQ, CTX_LEN), min(tq, CTX_LEN), CTX_LEN, unroll=1,
                    name=name + "_ctx")
    return jnp.concatenate([o_l, o_c], axis=0)


def _bd(x, m_left, m_right):
    return jnp.concatenate([x * m_left, x * m_right], axis=0)


def _cumsum_f32(tri, x):
    x1 = x.astype(BF16)
    r1 = x - x1.astype(F32)
    x2 = r1.astype(BF16)
    x3 = (r1 - x2.astype(F32)).astype(BF16)
    t = tri.astype(BF16)
    return (jnp.dot(t, x1, preferred_element_type=F32) + jnp.dot(t, x2, preferred_element_type=F32)
            + jnp.dot(t, x3, preferred_element_type=F32))


def _wkv_prep_kernel(r_ref, lw_ref, k_ref, v_ref, a_ref, b_ref,
                     wr_ref, u_ref, y0_ref, mrb_ref, bh_ref, g_ref, pc_ref, *, reverse):
    c = WKV_CHUNK
    n = RWKV_HEAD_DIM
    row = lax.broadcasted_iota(jnp.int32, (c, c), 0)
    col = lax.broadcasted_iota(jnp.int32, (c, c), 1)
    tri = ((col >= row) if reverse else (col <= row)).astype(F32)
    row2 = lax.broadcasted_iota(jnp.int32, (2 * c, 2 * c), 0) % c
    col2 = lax.broadcasted_iota(jnp.int32, (2 * c, 2 * c), 1) % c
    if reverse:
        incl, strict = col2 >= row2, col2 > row2
    else:
        incl, strict = col2 <= row2, col2 < row2
    lane = lax.broadcasted_iota(jnp.int32, (1, 2 * n), 1)
    m_left = (lane < n).astype(F32)
    m_right = 1.0 - m_left
    bd = functools.partial(_bd, m_left=m_left, m_right=m_right)
    last = 0 if reverse else c - 1

    chunks = range(WKV_PREP_CHUNKS)
    rows = [slice(ci * c, (ci + 1) * c) for ci in chunks]
    lw = [lw_ref[rw, :] for rw in rows]
    cum = [_cumsum_f32(tri, z) for z in lw]
    p_incl = [jnp.exp(z) for z in cum]
    p_inv = [jnp.exp(-z) for z in cum]
    pe = [z[last:last + 1, :] for z in p_incl]
    at = [bd(a_ref[rows[ci], :] * jnp.exp(cum[ci] - lw[ci])) for ci in chunks]
    rt = [bd(r_ref[rows[ci], :] * p_incl[ci]) for ci in chunks]
    bt = [bd(b_ref[rows[ci], :] * p_inv[ci]) for ci in chunks]
    kt = [bd(k_ref[rows[ci], :] * p_inv[ci]) for ci in chunks]
    v = [bd(v_ref[rows[ci], :]) for ci in chunks]
    big = [_dot_t(jnp.concatenate([at[ci], rt[ci]], axis=0), jnp.concatenate([bt[ci], kt[ci]], axis=0))
           for ci in chunks]
    l_pow = [jnp.where(strict, z[:2 * c, :2 * c], 0.0) for z in big]
    lm = [jnp.concatenate([jnp.where(strict, z[:2 * c, 2 * c:], 0.0),
                           jnp.where(incl, z[2 * c:, 2 * c:], 0.0)], axis=0) for z in big]
    for ci in chunks:
        wr_ref[ci, 0, 2 * c:, :] = rt[ci].astype(BF16)
        mrb_ref[ci, 0] = jnp.where(incl, big[ci][2 * c:, :2 * c], 0.0).astype(BF16)
        bh_ref[ci, 0] = (bt[ci] * pe[ci]).astype(BF16)
        pc_ref[ci, 0] = jnp.broadcast_to(pe[ci], (8, 2 * n))
    lv = [_dot(lm[ci], v[ci]) for ci in chunks]
    for ci in chunks:
        y0_ref[ci, 0] = lv[ci][2 * c:]
        g_ref[ci, 0] = _tdot(v[ci], kt[ci] * pe[ci])
    x = [jnp.concatenate([at[ci], lv[ci][:2 * c]], axis=1) for ci in chunks]
    x = [x[ci] + _dot(l_pow[ci], x[ci]) for ci in chunks]
    span = 1
    while span * 2 < c:
        l_pow = [_dot(z, z) for z in l_pow]
        x = [x[ci] + _dot(l_pow[ci], x[ci]) for ci in chunks]
        span *= 2
    for ci in chunks:
        wr_ref[ci, 0, :2 * c, :] = x[ci][:, :2 * n].astype(BF16)
        u_ref[ci, 0] = x[ci][:, 2 * n:]


def _wkv_chunk_fwd(s):
    return (s + SEQ // WKV_CHUNK) % WKV_CHUNKS


def _wkv_chunk_bwd(s):
    return WKV_CHUNKS - 1 - s


def wkv_prep(r, lw, k, v, a, b, reverse):
    c, n2 = WKV_CHUNK, 2 * RWKV_HEAD_DIM
    cb = WKV_PREP_CHUNKS
    spec = pl.BlockSpec((cb * c, n2), lambda p, i: (i, p))

    def out(rows, dtype):
        return (pl.BlockSpec((cb, 1, rows, n2), lambda p, i: (i, p, 0, 0)),
                jax.ShapeDtypeStruct((WKV_CHUNKS, WKV_PAIRS, rows, n2), dtype))

    outs = [out(4 * c, BF16), out(2 * c, F32), out(2 * c, F32), out(2 * c, BF16), out(2 * c, BF16),
            out(2 * c, F32), out(8, F32)]
    return pl.pallas_call(
        functools.partial(_wkv_prep_kernel, reverse=reverse),
        grid=(WKV_PAIRS, WKV_CHUNKS // cb),
        in_specs=[spec] * 6,
        out_specs=[o[0] for o in outs],
        out_shape=[o[1] for o in outs],
        compiler_params=_params("parallel", "parallel"),
        name="wkv_prep_bwd" if reverse else "wkv_prep_fwd",
    )(r, lw, k, v, a, b)


def _wkv_scan_kernel(*refs):
    c, n = WKV_CHUNK, RWKV_HEAD_DIM
    ins, (yf_ref, yb_ref, s_ref) = refs[:14], refs[14:]

    @pl.when(pl.program_id(0) == 0)
    def _():
        s_ref[...] = jnp.zeros_like(s_ref)

    jobs = [(d, p) for d in range(2) for p in range(WKV_PAIRS)]
    ref = lambda d, k: ins[7 * d + k]
    s0 = [s_ref[d, p] for d, p in jobs]
    z = [_dot_t(ref(d, 0)[0, p], s0[j]) for j, (d, p) in enumerate(jobs)]
    sa = [z[j][:2 * c] + ref(d, 1)[0, p] for j, (d, p) in enumerate(jobs)]
    for j, (d, p) in enumerate(jobs):
        s_ref[d, p] = s0[j] * ref(d, 6)[0, p, 0:1, :] + ref(d, 5)[0, p] + _tdot(sa[j], ref(d, 4)[0, p])
    for j, (d, p) in enumerate(jobs):
        y = z[j][2 * c:] + ref(d, 2)[0, p] + _dot(ref(d, 3)[0, p], sa[j])
        (yf_ref, yb_ref)[d][:, p * 2 * n:(p + 1) * 2 * n] = y[:c] + y[c:]


def wkv_scan(prep_f, prep_b):
    c, n2 = WKV_CHUNK, 2 * RWKV_HEAD_DIM

    def specs(chunk_of):
        return [pl.BlockSpec((1, WKV_PAIRS) + a.shape[2:], lambda s: (chunk_of(s), 0, 0, 0)) for a in prep_f]

    y_shape = jax.ShapeDtypeStruct((N_TOK, RWKV_WIDTH), F32)
    return pl.pallas_call(
        _wkv_scan_kernel,
        grid=(WKV_CHUNKS,),
        in_specs=specs(_wkv_chunk_fwd) + specs(_wkv_chunk_bwd),
        out_specs=[pl.BlockSpec((c, RWKV_WIDTH), lambda s: (_wkv_chunk_fwd(s), 0)),
                   pl.BlockSpec((c, RWKV_WIDTH), lambda s: (_wkv_chunk_bwd(s), 0))],
        out_shape=[y_shape, y_shape],
        scratch_shapes=[pltpu.VMEM((2, WKV_PAIRS, n2, n2), F32)],
        compiler_params=_params("arbitrary"),
        name="wkv_scan",
    )(*prep_f, *prep_b)


MOE_CAST_ROWS = 256


def _moe_kernel(be_ref, x_ref, w1_ref, b1_ref, w2_ref, b2_ref, g_ref, o_ref, w1b_ref, w2b_ref):
    i = pl.program_id(0)
    new_expert = (i == 0) | (be_ref[i] != be_ref[jnp.maximum(i - 1, 0)])

    @pl.when(new_expert)
    def _():
        def cast1(c, carry):
            rows = pl.ds(pl.multiple_of(c * MOE_CAST_ROWS, MOE_CAST_ROWS), MOE_CAST_ROWS)
            w1b_ref[rows, :] = w1_ref[0, rows, :].astype(BF16)
            return carry

        def cast2(c, carry):
            rows = pl.ds(pl.multiple_of(c * MOE_CAST_ROWS, MOE_CAST_ROWS), MOE_CAST_ROWS)
            w2b_ref[rows, :] = w2_ref[0, rows, :].astype(BF16)
            return carry

        lax.fori_loop(0, D_MODEL // MOE_CAST_ROWS, cast1, 0)
        lax.fori_loop(0, EXPERT_FF // MOE_CAST_ROWS, cast2, 0)

    u = jnp.dot(x_ref[...], w1b_ref[...], preferred_element_type=F32) + b1_ref[0]
    u_glu = jnp.minimum(u[:, :EXPERT_FF], SWIGLU_LIMIT)
    u_lin = jnp.clip(u[:, EXPERT_FF:], -SWIGLU_LIMIT, SWIGLU_LIMIT)
    act = u_glu * jax.nn.sigmoid(SWIGLU_ALPHA * u_glu) * (u_lin + 1.0)
    y = jnp.dot(act.astype(BF16), w2b_ref[...], preferred_element_type=F32) + b2_ref[0]
    o_ref[...] = y * g_ref[...]


def moe_experts(xg, slot_gate, block_expert, w1, b1, w2, b2, layer):
    n_slots = xg.shape[0]
    n_blocks = n_slots // DISPATCH_BLOCK
    grid_spec = pltpu.PrefetchScalarGridSpec(
        num_scalar_prefetch=1,
        grid=(n_blocks,),
        in_specs=[pl.BlockSpec((DISPATCH_BLOCK, D_MODEL), lambda i, be: (i, 0)),
                  pl.BlockSpec((None, 1, D_MODEL, 2 * EXPERT_FF), lambda i, be: (layer, be[i], 0, 0),
                               pipeline_mode=pl.Buffered(1)),
                  pl.BlockSpec((1, 1, 2 * EXPERT_FF), lambda i, be: (be[i], 0, 0)),
                  pl.BlockSpec((None, 1, EXPERT_FF, D_MODEL), lambda i, be: (layer, be[i], 0, 0),
                               pipeline_mode=pl.Buffered(1)),
                  pl.BlockSpec((1, 1, D_MODEL), lambda i, be: (be[i], 0, 0)),
                  pl.BlockSpec((DISPATCH_BLOCK, 1), lambda i, be: (i, 0))],
        out_specs=pl.BlockSpec((DISPATCH_BLOCK, D_MODEL), lambda i, be: (i, 0)),
        scratch_shapes=[pltpu.VMEM((D_MODEL, 2 * EXPERT_FF), BF16), pltpu.VMEM((EXPERT_FF, D_MODEL), BF16)],
    )
    return pl.pallas_call(
        _moe_kernel,
        grid_spec=grid_spec,
        out_shape=jax.ShapeDtypeStruct((n_slots, D_MODEL), F32),
        compiler_params=_params("arbitrary", vmem=MOE_VMEM_LIMIT),
        name="moe_experts",
    )(block_expert, xg, w1, b1.reshape(N_EXPERTS, 1, -1), w2, b2.reshape(N_EXPERTS, 1, -1),
      slot_gate.reshape(n_slots, 1))


def moe(tokens, tokens_bf16, router_w, router_b, w1, b1, w2, b2, layer):
    n_tok = tokens.shape[0]
    n_assign = n_tok * TOP_K
    n_blocks = -(-(n_assign + N_EXPERTS * (DISPATCH_BLOCK - 1)) // DISPATCH_BLOCK)
    n_slots = n_blocks * DISPATCH_BLOCK
    logits = router_logits(tokens, router_w, router_b)
    top_val, top_idx = lax.top_k(logits, TOP_K)
    gate = jax.nn.softmax(top_val, axis=-1).reshape(-1)
    expert = top_idx.reshape(-1)
    onehot = (expert[:, None] == jnp.arange(N_EXPERTS, dtype=expert.dtype)[None, :]).astype(jnp.int32)
    before = jnp.cumsum(onehot, axis=0) - onehot
    counts = before[-1] + onehot[-1]
    padded = (counts + DISPATCH_BLOCK - 1) // DISPATCH_BLOCK * DISPATCH_BLOCK
    pad_end = jnp.cumsum(padded)
    pad_start = pad_end - padded
    slot = jnp.sum(onehot * (before + pad_start[None, :]), axis=1).astype(jnp.int32)
    slot_assign = jnp.full((n_slots,), -1, jnp.int32).at[slot].set(jnp.arange(n_assign, dtype=jnp.int32))
    filled = slot_assign >= 0
    src = jnp.maximum(slot_assign, 0)
    slot_token = jnp.where(filled, src // TOP_K, 0)
    slot_gate = jnp.where(filled, gate[src], 0.0)
    block_start = jnp.arange(n_blocks, dtype=pad_end.dtype) * DISPATCH_BLOCK
    block_expert = jnp.minimum(jnp.sum(pad_end[None, :] <= block_start[:, None], axis=1),
                               N_EXPERTS - 1).astype(jnp.int32)
    xg = tokens_bf16[slot_token]
    y = moe_experts(xg, slot_gate, block_expert, w1, b1, w2, b2, layer)
    return jnp.sum(y[slot.reshape(n_tok, TOP_K)], axis=1)


def _rope_lanes(x, cos, sin, half):
    lane = lax.broadcasted_iota(jnp.int32, x.shape, 1)
    up = pltpu.roll(x, 128 - half, 1)
    down = pltpu.roll(x, half, 1)
    return x * cos + jnp.where(lane % (2 * half) < half, up, down) * sin


def _rms(x, g):
    return x * lax.rsqrt(jnp.mean(x * x, axis=-1, keepdims=True) + RMS_EPS) * g


def _attn_prep_kernel(pa_ref, pb_ref, pc_ref, gq_ref, gk_ref, gmq_ref, gmkv_ref, cosa_ref, sina_ref,
                      cosb_ref, sinb_ref, wuq_ref, wukv_ref, qg_ref, kg_ref, vg_ref, qc_ref, kc_ref, vm_ref):
    hd = GQA_HEAD_DIM
    ca, sa, cb, sb = cosa_ref[...], sina_ref[...], cosb_ref[...], sinb_ref[...]
    grp = GQA_HEADS // GQA_KV_HEADS
    for h in range(GQA_HEADS):
        q = _rope_lanes(_rms(pa_ref[:, h * hd:(h + 1) * hd], gq_ref[...]), ca, sa, hd // 4)
        qg_ref[h // grp, h % grp] = (q * GQA_HEAD_DIM ** -0.5).astype(BF16)
    for h in range(GQA_KV_HEADS):
        k = _rope_lanes(_rms(pb_ref[:, h * hd:(h + 1) * hd], gk_ref[...]), ca, sa, hd // 4)
        kg_ref[h] = k.astype(BF16)
        vg_ref[h] = pb_ref[:, (GQA_KV_HEADS + h) * hd:(GQA_KV_HEADS + h + 1) * hd].astype(BF16)
    dkvn = _rms(pb_ref[:, 2 * GQA_KV_HEADS * hd:], gmkv_ref[...])
    dqn = _rms(pc_ref[:, :MLA_Q_RANK], gmq_ref[...])
    k_rope = _rope_lanes(pc_ref[:, MLA_Q_RANK:], cb, sb, MLA_ROPE_DIM // 4)
    qb = _dot(dqn, wuq_ref[...])
    kvb = _dot(dkvn, wukv_ref[...])
    scale = MLA_QK_DIM ** -0.5
    for h in range(MLA_HEADS):
        base = h * MLA_CAT_DIM
        q_nope = qb[:, base:base + 128]
        q_rope = _rope_lanes(qb[:, base + 128:base + 256], cb, sb, MLA_ROPE_DIM // 4)
        qc_ref[h, 0] = (jnp.concatenate([q_nope, q_rope], axis=1) * scale).astype(BF16)
        kc_ref[h] = jnp.concatenate([kvb[:, base:base + 128], k_rope], axis=1).astype(BF16)
        vm_ref[h] = kvb[:, base + 128:base + 256].astype(BF16)


def _rope_lane_tables(rot_dim):
    n_freq = rot_dim // 4
    row = jnp.repeat(jnp.arange(SEQ // GRID_W, dtype=F32), GRID_W)
    col = (jnp.arange(SEQ) % GRID_W).astype(F32)
    freqs = ROPE_THETA ** (-jnp.arange(n_freq, dtype=F32) / n_freq)
    ar, ac = row[:, None] * freqs, col[:, None] * freqs
    cos = jnp.concatenate([jnp.cos(ar), jnp.cos(ar), jnp.cos(ac), jnp.cos(ac)], axis=1)
    sin = jnp.concatenate([-jnp.sin(ar), jnp.sin(ar), -jnp.sin(ac), jnp.sin(ac)], axis=1)
    cos = jnp.pad(cos, ((0, CTX_LEN), (0, 128 - rot_dim)), constant_values=1.0)
    sin = jnp.pad(sin, ((0, CTX_LEN), (0, 128 - rot_dim)))
    return cos, sin


def attn_prep(proj, gqa_q_norm, gqa_k_norm, mla_q_norm, mla_kv_norm, mla_w_uq, mla_w_ukv):
    t = N_TOK
    cos_a, sin_a = _rope_lane_tables(GQA_HEAD_DIM)
    cos_b, sin_b = _rope_lane_tables(MLA_ROPE_DIM)
    w_uq = jnp.pad(mla_w_uq.reshape(MLA_Q_RANK, MLA_HEADS, MLA_QK_DIM),
                   ((0, 0), (0, 0), (0, MLA_CAT_DIM - MLA_QK_DIM))).reshape(MLA_Q_RANK, -1).astype(BF16)
    w_ukv = mla_w_ukv.astype(BF16)
    first = SMALL_COL0 // 1024
    col = lambda j: pl.BlockSpec((ROW_TILE, 1024), lambda i: (i, first + j))
    vec = lambda n: pl.BlockSpec((1, n), lambda i: (0, 0))
    tab = pl.BlockSpec((ROW_TILE, 128), lambda i: (i, 0))
    full = lambda a: pl.BlockSpec(a.shape, lambda i: (0, 0))
    grp = GQA_HEADS // GQA_KV_HEADS
    hd = GQA_HEAD_DIM
    outs = [
        (pl.BlockSpec((GQA_KV_HEADS, grp, ROW_TILE, hd), lambda i: (0, 0, i, 0)), (GQA_KV_HEADS, grp, t, hd)),
        (pl.BlockSpec((GQA_KV_HEADS, ROW_TILE, hd), lambda i: (0, i, 0)), (GQA_KV_HEADS, t, hd)),
        (pl.BlockSpec((GQA_KV_HEADS, ROW_TILE, hd), lambda i: (0, i, 0)), (GQA_KV_HEADS, t, hd)),
        (pl.BlockSpec((MLA_HEADS, 1, ROW_TILE, MLA_CAT_DIM), lambda i: (0, 0, i, 0)), (MLA_HEADS, 1, t, MLA_CAT_DIM)),
        (pl.BlockSpec((MLA_HEADS, ROW_TILE, MLA_CAT_DIM), lambda i: (0, i, 0)), (MLA_HEADS, t, MLA_CAT_DIM)),
        (pl.BlockSpec((MLA_HEADS, ROW_TILE, MLA_V_DIM), lambda i: (0, i, 0)), (MLA_HEADS, t, MLA_V_DIM)),
    ]
    return pl.pallas_call(
        _attn_prep_kernel,
        grid=(t // ROW_TILE,),
        in_specs=[col(0), col(1), col(2), vec(hd), vec(hd), vec(MLA_Q_RANK), vec(MLA_KV_RANK),
                  tab, tab, tab, tab, full(w_uq), full(w_ukv)],
        out_specs=[o[0] for o in outs],
        out_shape=[jax.ShapeDtypeStruct(o[1], BF16) for o in outs],
        compiler_params=_params("parallel"),
        name="attn_prep",
    )(proj, proj, proj, gqa_q_norm.reshape(1, -1), gqa_k_norm.reshape(1, -1), mla_q_norm.reshape(1, -1),
      mla_kv_norm.reshape(1, -1), cos_a, sin_a, cos_b, sin_b, w_uq, w_ukv)


def _group_sum(x, ones_bd):
    hi = x.astype(BF16)
    lo = (x - hi.astype(F32)).astype(BF16)
    return jnp.dot(hi, ones_bd, preferred_element_type=F32) + jnp.dot(lo, ones_bd, preferred_element_type=F32)


def _softplus(z):
    return jnp.maximum(z, 0.0) + jnp.log(1.0 + jnp.exp(-jnp.abs(z)))


def _rwkv_prep_kernel(p_ref, prev_ref, next_ref, mu_ref, wup_ref, w0_ref, aup_ref, a0_ref, gup_ref,
                      kk_ref, ka_ref, rk_ref, ones_ref,
                      r_ref, v_ref, a_ref, lwf_ref, kdf_ref, bf_ref, lwb_ref, kdb_ref, bb_ref, g_ref, bonus_ref):
    i = pl.program_id(0)
    w = RWKV_WIDTH
    p = p_ref[...]
    n = p.shape[0]
    rowid = lax.broadcasted_iota(jnp.int32, (n, 1), 0)
    has_prev = jnp.where(i % LAT_TILES == 0, 0.0, 1.0)
    has_next = jnp.where((i == LAT_TILES - 1) | (i == LAT_TILES), 0.0, 1.0)
    prev = jnp.where(rowid == 0, prev_ref[7:8, :] * has_prev, pltpu.roll(p, 1, 0))
    nxt = jnp.where(rowid == n - 1, next_ref[0:1, :] * has_next, pltpu.roll(p, n - 1, 0))
    p = p + mu_ref[0:1, :] * (prev - p) + mu_ref[1:2, :] * (nxt - p)

    ones_bd = ones_ref[...]
    r, kx, vx = p[:, :w], p[:, w:2 * w], p[:, 2 * w:3 * w]
    low = p[:, 3 * w:]
    g_ref[...] = _dot(jax.nn.sigmoid(low[:, 512:]), gup_ref[...])
    kk = kx * kk_ref[...]
    kk = kk / jnp.maximum(jnp.sqrt(_group_sum(kk * kk, ones_bd)), 1e-12)
    r_ref[...] = r
    v_ref[...] = vx
    a_ref[...] = -kk
    k_sum = jnp.zeros_like(kx)
    for d, (lw_ref, kd_ref, b_ref) in enumerate(((lwf_ref, kdf_ref, bf_ref), (lwb_ref, kdb_ref, bb_ref))):
        wd = low[:, d * 128:(d + 1) * 128]
        ad = low[:, 256 + d * 128:256 + (d + 1) * 128]
        w_lin = _dot(jnp.tanh(wd), wup_ref[d]) + w0_ref[d:d + 1, :]
        lw_ref[...] = -jnp.exp(-_softplus(-w_lin) - 0.5)
        a_gate = jax.nn.sigmoid(_dot(ad, aup_ref[d]) + a0_ref[d:d + 1, :])
        k_d = kx * (1.0 + (a_gate - 1.0) * ka_ref[...])
        kd_ref[...] = k_d
        b_ref[...] = kk * a_gate
        k_sum = k_sum + k_d
    bonus_ref[...] = _group_sum(r * k_sum * rk_ref[...], ones_bd) * vx


def _head_ones():
    h = jnp.arange(RWKV_WIDTH) // RWKV_HEAD_DIM
    return (h[:, None] == h[None, :]).astype(BF16)


def rwkv_prep(proj, rwkv_mu, rwkv_w0, rwkv_w_up, rwkv_a0, rwkv_a_up, rwkv_g_up, rwkv_k_k, rwkv_k_a, rwkv_r_k):
    t, w = N_TOK, RWKV_WIDTH
    cb = RWKV_COL0 // RWKV_IN_PAD
    n8 = t // 8
    per8 = ROW_TILE // 8
    mu = jnp.pad(rwkv_mu, ((0, 6), (0, RWKV_IN_PAD - RWKV_IN)))
    g_up = jnp.pad(rwkv_g_up, ((0, 512 - RWKV_GATE_RANK), (0, 0))).astype(BF16)
    full = lambda a: pl.BlockSpec(a.shape, lambda i: (0,) * a.ndim)
    consts = [mu, rwkv_w_up.astype(BF16), rwkv_w0, rwkv_a_up.astype(BF16), rwkv_a0, g_up,
              rwkv_k_k.reshape(1, w), rwkv_k_a.reshape(1, w), rwkv_r_k.reshape(1, w), _head_ones()]
    out_spec = pl.BlockSpec((ROW_TILE, w), lambda i: (i, 0))
    return pl.pallas_call(
        _rwkv_prep_kernel,
        grid=(t // ROW_TILE,),
        in_specs=[pl.BlockSpec((ROW_TILE, RWKV_IN_PAD), lambda i: (i, cb)),
                  pl.BlockSpec((8, RWKV_IN_PAD), lambda i: (jnp.maximum(i * per8 - 1, 0), cb)),
                  pl.BlockSpec((8, RWKV_IN_PAD), lambda i: (jnp.minimum((i + 1) * per8, n8 - 1), cb))]
                 + [full(a) for a in consts],
        out_specs=[out_spec] * 11,
        out_shape=[jax.ShapeDtypeStruct((t, w), F32)] * 11,
        compiler_params=_params("parallel"),
        name="rwkv_prep",
    )(proj, proj, proj, *consts)


def _rwkv_out_kernel(yf_ref, yb_ref, bonus_ref, g_ref, gng_ref, gnb_ref, ones_ref, o_ref):
    ones_bd = ones_ref[...]
    y = yf_ref[...] + yb_ref[...]
    yc = y - _group_sum(y, ones_bd) * (1.0 / RWKV_HEAD_DIM)
    var = _group_sum(yc * yc, ones_bd) * (1.0 / RWKV_HEAD_DIM)
    yn = yc * lax.rsqrt(var + GN_EPS) * gng_ref[...] + gnb_ref[...]
    o_ref[...] = ((yn + bonus_ref[...]) * g_ref[...]).astype(o_ref.dtype)


def rwkv_out(y_f, y_b, bonus, g, gn_g, gn_b):
    t, w = N_TOK, RWKV_WIDTH
    row = pl.BlockSpec((ROW_TILE, w), lambda i: (i, 0))
    vec = pl.BlockSpec((1, w), lambda i: (0, 0))
    ones = _head_ones()
    return pl.pallas_call(
        _rwkv_out_kernel,
        grid=(t // ROW_TILE,),
        in_specs=[row, row, row, row, vec, vec, pl.BlockSpec(ones.shape, lambda i: (0, 0))],
        out_specs=row,
        out_shape=jax.ShapeDtypeStruct((t, w), BF16),
        compiler_params=_params("parallel"),
        name="rwkv_out",
    )(y_f, y_b, bonus, g, gn_g.reshape(1, w), gn_b.reshape(1, w), ones)


def _merge_kernel(oa_ref, ob_ref, oc_ref, wa_ref, wb_ref, wc_ref, ga_ref, gb_ref, gc_ref, m_ref):
    acc = jax.nn.sigmoid(ga_ref[...]) * _dot(oa_ref[...], wa_ref[0])
    acc += jax.nn.sigmoid(gb_ref[...]) * _dot(ob_ref[...], wb_ref[0])
    acc += jax.nn.sigmoid(gc_ref[...]) * _dot(oc_ref[...], wc_ref[0])
    m_ref[...] = acc.astype(m_ref.dtype)


def merge(oa, ob, oc, proj, w_branch_up):
    t = N_TOK
    tm, tn = 768, 512
    nj = D_MODEL // tn
    wb = w_branch_up.astype(BF16)
    o_spec = pl.BlockSpec((tm, BRANCH_WIDTH), lambda i, j: (i, 0))
    w_spec = lambda b: pl.BlockSpec((1, BRANCH_WIDTH, tn), lambda i, j: (b, 0, j))
    g_spec = lambda b: pl.BlockSpec((tm, tn), lambda i, j: (i, b * nj + j))
    return pl.pallas_call(
        _merge_kernel,
        grid=(t // tm, nj),
        in_specs=[o_spec, o_spec, o_spec, w_spec(0), w_spec(1), w_spec(2), g_spec(0), g_spec(1), g_spec(2)],
        out_specs=pl.BlockSpec((tm, tn), lambda i, j: (i, j)),
        out_shape=jax.ShapeDtypeStruct((t, D_MODEL), BF16),
        compiler_params=_params("parallel", "arbitrary"),
        name="merge",
    )(oa, ob, oc, wb, wb, wb, proj, proj, proj)


def _split(x, sizes):
    return jnp.split(x, np.cumsum(sizes)[:-1].tolist(), axis=-1)


def _in_proj_weight(w_in):
    aq, ak, av, dq, dkv, kr, rwkv, gates = _split(w_in, IN_SPLITS)
    small = jnp.concatenate([aq, ak, av, dkv, dq, kr], axis=1)
    pad = lambda z, n: jnp.pad(z, ((0, 0), (0, n - z.shape[1])))
    return jnp.concatenate([gates, pad(rwkv, RWKV_IN_PAD), pad(small, SMALL_WIDTH)], axis=1).astype(BF16)


def _mixer(h, w_in, gqa_q_norm, gqa_k_norm, mla_q_norm, mla_kv_norm, mla_w_uq, mla_w_ukv,
           rwkv_mu, rwkv_w0, rwkv_w_up, rwkv_a0, rwkv_a_up, rwkv_g_up, rwkv_k_k, rwkv_k_a, rwkv_r_k,
           rwkv_gn_g, rwkv_gn_b, w_branch_up, w_o):
    proj = matmul(h, _in_proj_weight(w_in), name="in_proj")

    qg, kg, vg, qc, kc, vm = attn_prep(proj, gqa_q_norm, gqa_k_norm, mla_q_norm, mla_kv_norm, mla_w_uq, mla_w_ukv)
    oa = lat_ctx_attention(qg, kg, vg, 128, "gqa")
    ob = lat_ctx_attention(qc, kc, vm, 512, "mla")

    r, v, a, lw_f, kd_f, b_f, lw_b, kd_b, b_b, g, bonus = rwkv_prep(
        proj, rwkv_mu, rwkv_w0, rwkv_w_up, rwkv_a0, rwkv_a_up, rwkv_g_up, rwkv_k_k, rwkv_k_a, rwkv_r_k)
    y_f, y_b = wkv_scan(wkv_prep(r, lw_f, kd_f, v, a, b_f, reverse=False),
                        wkv_prep(r, lw_b, kd_b, v, a, b_b, reverse=True))
    oc = rwkv_out(y_f, y_b, bonus, g, rwkv_gn_g, rwkv_gn_b)

    m = merge(oa, ob, oc, proj, w_branch_up)
    return matmul(m, w_o.astype(BF16), name="out_proj")


def kernel(x, c, ctx, c_ctx, w_ada, b_ada, w_in, gqa_q_norm, gqa_k_norm, mla_q_norm, mla_kv_norm, mla_w_uq,
           mla_w_ukv, rwkv_mu, rwkv_w0, rwkv_w_up, rwkv_a0, rwkv_a_up, rwkv_g_up, rwkv_k_k, rwkv_k_a, rwkv_r_k,
           rwkv_gn_g, rwkv_gn_b, w_branch_up, w_o, ln1_g, ln1_b, router_w, router_b, exp_w_in, exp_b_in,
           exp_w_out, exp_b_out, ln2_g, ln2_b):
    xs = jnp.concatenate([x[0], ctx[0]], axis=0)
    cond = jnp.zeros((16, D_MODEL), F32).at[0].set(jax.nn.silu(c[0])).at[1].set(jax.nn.silu(c_ctx))
    for l in range(DEPTH):
        ada = matmul(cond.astype(BF16), w_ada, bias=b_ada[l], tm=16, tn=512, layer=l, name="adaln")
        sh1, sc1, g1, sh2, sc2, g2 = (z[:2].reshape(2, 1, D_MODEL) for z in _split(ada, (D_MODEL,) * 6))
        h = modulate(xs, sc1, sh1)
        m = _mixer(h, w_in[l], gqa_q_norm[l], gqa_k_norm[l], mla_q_norm[l], mla_kv_norm[l], mla_w_uq[l],
                   mla_w_ukv[l], rwkv_mu[l], rwkv_w0[l], rwkv_w_up[l], rwkv_a0[l], rwkv_a_up[l], rwkv_g_up[l],
                   rwkv_k_k[l], rwkv_k_a[l], rwkv_r_k[l], rwkv_gn_g[l], rwkv_gn_b[l], w_branch_up[l], w_o[l])
        xs, tokens, tokens_bf16 = post_ln(xs, m, g1, ln1_g[l], ln1_b[l], sc2, sh2)
        f = moe(tokens, tokens_bf16, router_w[l], router_b[l], exp_w_in, exp_b_in[l], exp_w_out,
                exp_b_out[l], l)
        xs = post_ln(xs, f, g2, ln2_g[l], ln2_b[l])
    return xs[:SEQ].reshape(1, SEQ, D_MODEL)
```

```python
import functools

import jax
import jax.numpy as jnp
import numpy as np
from jax import lax
from jax.experimental import pallas as pl
from jax.experimental.pallas import tpu as pltpu

F32, BF16 = jnp.float32, jnp.bfloat16

D_MODEL = 4096
SEQ = 8192
CTX_LEN = 256
N_TOK = SEQ + CTX_LEN
DEPTH = 2
GRID_W = 64
ROPE_THETA = 10000.0

GQA_HEADS, GQA_KV_HEADS, GQA_HEAD_DIM = 8, 2, 128
MLA_HEADS, MLA_Q_RANK, MLA_KV_RANK = 8, 896, 512
MLA_NOPE_DIM, MLA_ROPE_DIM, MLA_V_DIM = 128, 64, 128
MLA_QK_DIM = MLA_NOPE_DIM + MLA_ROPE_DIM
MLA_CAT_DIM = 256
RWKV_HEADS, RWKV_HEAD_DIM = 16, 64
RWKV_WIDTH = RWKV_HEADS * RWKV_HEAD_DIM
RWKV_DECAY_RANK, RWKV_ICLR_RANK, RWKV_GATE_RANK = 128, 128, 480
RWKV_SPLITS = (RWKV_WIDTH, RWKV_WIDTH, RWKV_WIDTH, RWKV_DECAY_RANK, RWKV_DECAY_RANK,
               RWKV_ICLR_RANK, RWKV_ICLR_RANK, RWKV_GATE_RANK)
RWKV_IN = sum(RWKV_SPLITS)
RWKV_IN_PAD = 4096
BRANCH_WIDTH = 1024
N_BRANCHES = 3
N_EXPERTS, TOP_K, EXPERT_FF = 32, 4, 512
SWIGLU_ALPHA, SWIGLU_LIMIT = 1.702, 7.0
DISPATCH_BLOCK = 128
DEEPNORM_ALPHA = (2 * DEPTH) ** 0.25
LN_EPS, RMS_EPS, GN_EPS = 1e-5, 1e-6, 64e-5

IN_SPLITS = (1024, 256, 256, MLA_Q_RANK, MLA_KV_RANK, MLA_ROPE_DIM, RWKV_IN, N_BRANCHES * D_MODEL)
SMALL_SPLITS = IN_SPLITS[:6]
SMALL_WIDTH = 3072
RWKV_COL0 = N_BRANCHES * D_MODEL
SMALL_COL0 = RWKV_COL0 + RWKV_IN_PAD

ROW_TILE = 256
LAT_TILES = SEQ // ROW_TILE
WKV_CHUNK = 64
WKV_PAIRS = RWKV_HEADS // 2
WKV_CHUNKS = N_TOK // WKV_CHUNK
WKV_CTX_CHUNKS = CTX_LEN // WKV_CHUNK
WKV_PREP_CHUNKS = 12
ATTN_KV_TILE = 768
TOKEN_SLAB = (16, 256)
EXPERT_SLAB = (8, 512)
VMEM_LIMIT = 48 * 1024 * 1024
MOE_VMEM_LIMIT = 56 * 1024 * 1024


def _params(*sem, vmem=VMEM_LIMIT):
    return pltpu.CompilerParams(dimension_semantics=sem, vmem_limit_bytes=vmem)


def _dot(a, b):
    return jnp.dot(a.astype(BF16), b.astype(BF16), preferred_element_type=F32)


def _dot_t(a, b):
    return lax.dot_general(a.astype(BF16), b.astype(BF16), (((1,), (1,)), ((), ())), preferred_element_type=F32)


def _tdot(a, b):
    return lax.dot_general(a.astype(BF16), b.astype(BF16), (((0,), (0,)), ((), ())), preferred_element_type=F32)


def _mm_kernel(a_ref, b_ref, bias_ref, o_ref):
    o_ref[...] = (_dot(a_ref[...], b_ref[...]) + bias_ref[...]).astype(o_ref.dtype)


def _pick(n, prefs):
    for p in prefs:
        if n % p == 0:
            return p
    return n


def matmul(a, b, bias=None, out_dtype=F32, tm=None, tn=None, layer=None, name="matmul"):
    m, kdim = a.shape
    n = b.shape[-1]
    tm = tm or _pick(m, (768, 512, 256, 128, 16))
    tn = tn or _pick(n, (512, 256, 128))
    if bias is None:
        bias = jnp.zeros((n,), F32)
    bias = bias.reshape(1, n).astype(F32)
    if layer is None:
        b_spec = pl.BlockSpec((kdim, tn), lambda i, j: (0, j))
    else:
        b_spec = pl.BlockSpec((None, kdim, tn), lambda i, j: (layer, 0, j))
    return pl.pallas_call(
        _mm_kernel,
        grid=(m // tm, n // tn),
        in_specs=[pl.BlockSpec((tm, kdim), lambda i, j: (i, 0)),
                  b_spec,
                  pl.BlockSpec((1, tn), lambda i, j: (0, j))],
        out_specs=pl.BlockSpec((tm, tn), lambda i, j: (i, j)),
        out_shape=jax.ShapeDtypeStruct((m, n), out_dtype),
        compiler_params=_params("parallel", "arbitrary"),
        name=name,
    )(a, b, bias)


def _router_kernel(x_ref, w_ref, b_ref, o_ref):
    x = x_ref[...]
    w = w_ref[...]
    xh = x.astype(BF16)
    xl = (x - xh.astype(F32)).astype(BF16)
    wh = w.astype(BF16)
    wl = (w - wh.astype(F32)).astype(BF16)
    acc = jnp.dot(xh, wh, preferred_element_type=F32)
    acc += jnp.dot(xh, wl, preferred_element_type=F32)
    acc += jnp.dot(xl, wh, preferred_element_type=F32)
    o_ref[...] = acc + b_ref[...]


def router_logits(tokens, router_w, router_b):
    n_tok = tokens.shape[0]
    npad = 128
    w = jnp.zeros((D_MODEL, npad), F32).at[:, :N_EXPERTS].set(router_w)
    b = jnp.zeros((1, npad), F32).at[0, :N_EXPERTS].set(router_b)
    out = pl.pallas_call(
        _router_kernel,
        grid=(n_tok // ROW_TILE,),
        in_specs=[pl.BlockSpec((ROW_TILE, D_MODEL), lambda i: (i, 0)),
                  pl.BlockSpec((D_MODEL, npad), lambda i: (0, 0)),
                  pl.BlockSpec((1, npad), lambda i: (0, 0))],
        out_specs=pl.BlockSpec((ROW_TILE, npad), lambda i: (i, 0)),
        out_shape=jax.ShapeDtypeStruct((n_tok, npad), F32),
        compiler_params=_params("parallel"),
        name="router",
    )(tokens, w, b)
    return out[:, :N_EXPERTS]


def _mod_index(i):
    return (i // LAT_TILES, 0, 0)


def _modulate_kernel(x_ref, sc_ref, sh_ref, o_ref):
    o_ref[...] = (x_ref[...] * (1.0 + sc_ref[0]) + sh_ref[0]).astype(o_ref.dtype)


def modulate(x, sc, sh, out_dtype=BF16):
    n = x.shape[0]
    return pl.pallas_call(
        _modulate_kernel,
        grid=(n // ROW_TILE,),
        in_specs=[pl.BlockSpec((ROW_TILE, D_MODEL), lambda i: (i, 0)),
                  pl.BlockSpec((1, 1, D_MODEL), _mod_index),
                  pl.BlockSpec((1, 1, D_MODEL), _mod_index)],
        out_specs=pl.BlockSpec((ROW_TILE, D_MODEL), lambda i: (i, 0)),
        out_shape=jax.ShapeDtypeStruct((n, D_MODEL), out_dtype),
        compiler_params=_params("parallel"),
        name="modulate",
    )(x, sc, sh)


def _layernorm(z, g, b):
    mu = jnp.mean(z, axis=-1, keepdims=True)
    zc = z - mu
    var = jnp.mean(zc * zc, axis=-1, keepdims=True)
    return zc * lax.rsqrt(var + LN_EPS) * g + b


def _postln_mod_kernel(x_ref, m_ref, g_ref, sc_ref, sh_ref, lng_ref, lnb_ref, y_ref, t_ref, tb_ref):
    y = _layernorm(DEEPNORM_ALPHA * x_ref[...] + g_ref[0] * m_ref[...], lng_ref[...], lnb_ref[...])
    y_ref[...] = y
    t = y * (1.0 + sc_ref[0]) + sh_ref[0]
    t_ref[...] = t
    for s in range(TOKEN_SLAB[0]):
        tb_ref[:, s, :] = t[:, s * TOKEN_SLAB[1]:(s + 1) * TOKEN_SLAB[1]].astype(BF16)


def post_ln(x, m, g, ln_g, ln_b, sc, sh):
    n = x.shape[0]
    row = pl.BlockSpec((ROW_TILE, D_MODEL), lambda i: (i, 0))
    mod = pl.BlockSpec((1, 1, D_MODEL), _mod_index)
    vec = pl.BlockSpec((1, D_MODEL), lambda i: (0, 0))
    f32_rows = jax.ShapeDtypeStruct((n, D_MODEL), F32)
    return pl.pallas_call(
        _postln_mod_kernel,
        grid=(n // ROW_TILE,),
        in_specs=[row, row, mod, mod, mod, vec, vec],
        out_specs=[row, row, pl.BlockSpec((ROW_TILE,) + TOKEN_SLAB, lambda i: (i, 0, 0))],
        out_shape=[f32_rows, f32_rows, jax.ShapeDtypeStruct((n,) + TOKEN_SLAB, BF16)],
        compiler_params=_params("parallel"),
        name="post_ln_mod",
    )(x, m, g, sc, sh, ln_g.reshape(1, D_MODEL), ln_b.reshape(1, D_MODEL))


def _attn_kernel(q_ref, k_ref, v_ref, o_ref, *, nkv, tk, unroll):
    g, tq, d = q_ref.shape[1:]
    dv = v_ref.shape[-1]
    rows = g * tq
    q = q_ref[0].reshape(rows, d)

    def body(j, carry):
        m_prev, l_prev, acc = carry
        start = pl.multiple_of(j * tk, tk)
        kj = k_ref[0, pl.ds(start, tk), :]
        vj = v_ref[0, pl.ds(start, tk), :]
        s = lax.dot_general(q, kj, (((1,), (1,)), ((), ())), preferred_element_type=F32)
        m_new = jnp.maximum(m_prev, jnp.max(s, axis=-1, keepdims=True))
        alpha = jnp.exp(m_prev - m_new)
        p = jnp.exp(s - m_new)
        l_new = alpha * l_prev + jnp.sum(p, axis=-1, keepdims=True)
        acc = alpha * acc + jnp.dot(p.astype(BF16), vj, preferred_element_type=F32)
        return m_new, l_new, acc

    init = (jnp.full((rows, 1), -jnp.inf, F32), jnp.zeros((rows, 1), F32), jnp.zeros((rows, dv), F32))
    _, l, acc = lax.fori_loop(0, nkv, body, init, unroll=unroll)
    o = acc / l
    for gi in range(g):
        o_ref[:, gi * dv:(gi + 1) * dv] = o[gi * tq:(gi + 1) * tq].astype(o_ref.dtype)


def attention(q, k, v, q_rows, kv_rows, tq, tk, unroll=2, name="attn"):
    hkv, g, _, d = q.shape
    dv = v.shape[-1]
    q0, nq = q_rows
    k0, nk = kv_rows
    assert q0 % tq == 0 and nq % tq == 0 and k0 % nk == 0 and nk % tk == 0
    qb, kb = q0 // tq, k0 // nk
    return pl.pallas_call(
        functools.partial(_attn_kernel, nkv=nk // tk, tk=tk, unroll=unroll),
        grid=(hkv, nq // tq),
        in_specs=[pl.BlockSpec((1, g, tq, d), lambda h, i: (h, 0, i + qb, 0)),
                  pl.BlockSpec((1, nk, d), lambda h, i: (h, kb, 0)),
                  pl.BlockSpec((1, nk, dv), lambda h, i: (h, kb, 0))],
        out_specs=pl.BlockSpec((tq, g * dv), lambda h, i: (i, h)),
        out_shape=jax.ShapeDtypeStruct((nq, hkv * g * dv), BF16),
        compiler_params=_params("parallel", "arbitrary"),
        name=name,
    )(q, k, v)


def lat_ctx_attention(q, k, v, tq, name):
    o_l = attention(q, k, v, (0, SEQ), (0, N_TOK), tq, ATTN_KV_TILE, unroll=N_TOK // ATTN_KV_TILE,
                    name=name + "_lat")
    o_c = attention(q, k, v, (SEQ, CTX_LEN), (SEQ, CTX_LEN), min(tq, CTX_LEN), CTX_LEN, unroll=1,
                    name=name + "_ctx")
    return jnp.concatenate([o_l, o_c], axis=0)


def _bd(x, m_left, m_right):
    return jnp.concatenate([x * m_left, x * m_right], axis=0)


def _cumsum_f32(tri, x):
    x1 = x.astype(BF16)
    r1 = x - x1.astype(F32)
    x2 = r1.astype(BF16)
    x3 = (r1 - x2.astype(F32)).astype(BF16)
    t = tri.astype(BF16)
    return (jnp.dot(t, x1, preferred_element_type=F32) + jnp.dot(t, x2, preferred_element_type=F32)
            + jnp.dot(t, x3, preferred_element_type=F32))


def _wkv_prep_kernel(r_ref, lw_ref, k_ref, v_ref, a_ref, b_ref,
                     wr_ref, u_ref, y0_ref, mrb_ref, bh_ref, g_ref, pc_ref, *, reverse):
    c = WKV_CHUNK
    n = RWKV_HEAD_DIM
    row = lax.broadcasted_iota(jnp.int32, (c, c), 0)
    col = lax.broadcasted_iota(jnp.int32, (c, c), 1)
    tri = ((col >= row) if reverse else (col <= row)).astype(F32)
    row2 = lax.broadcasted_iota(jnp.int32, (2 * c, 2 * c), 0) % c
    col2 = lax.broadcasted_iota(jnp.int32, (2 * c, 2 * c), 1) % c
    if reverse:
        incl, strict = col2 >= row2, col2 > row2
    else:
        incl, strict = col2 <= row2, col2 < row2
    lane = lax.broadcasted_iota(jnp.int32, (1, 2 * n), 1)
    m_left = (lane < n).astype(F32)
    m_right = 1.0 - m_left
    bd = functools.partial(_bd, m_left=m_left, m_right=m_right)
    last = 0 if reverse else c - 1

    chunks = range(WKV_PREP_CHUNKS)
    rows = [slice(ci * c, (ci + 1) * c) for ci in chunks]
    lw = [lw_ref[rw, :] for rw in rows]
    cum = [_cumsum_f32(tri, z) for z in lw]
    p_incl = [jnp.exp(z) for z in cum]
    p_inv = [jnp.exp(-z) for z in cum]
    pe = [z[last:last + 1, :] for z in p_incl]
    at = [bd(a_ref[rows[ci], :] * jnp.exp(cum[ci] - lw[ci])) for ci in chunks]
    rt = [bd(r_ref[rows[ci], :] * p_incl[ci]) for ci in chunks]
    bt = [bd(b_ref[rows[ci], :] * p_inv[ci]) for ci in chunks]
    kt = [bd(k_ref[rows[ci], :] * p_inv[ci]) for ci in chunks]
    v = [bd(v_ref[rows[ci], :]) for ci in chunks]
    big = [_dot_t(jnp.concatenate([at[ci], rt[ci]], axis=0), jnp.concatenate([bt[ci], kt[ci]], axis=0))
           for ci in chunks]
    l_pow = [jnp.where(strict, z[:2 * c, :2 * c], 0.0) for z in big]
    lm = [jnp.concatenate([jnp.where(strict, z[:2 * c, 2 * c:], 0.0),
                           jnp.where(incl, z[2 * c:, 2 * c:], 0.0)], axis=0) for z in big]
    for ci in chunks:
        wr_ref[ci, 0, 2 * c:, :] = rt[ci].astype(BF16)
        mrb_ref[ci, 0] = jnp.where(incl, big[ci][2 * c:, :2 * c], 0.0).astype(BF16)
        bh_ref[ci, 0] = (bt[ci] * pe[ci]).astype(BF16)
        pc_ref[ci, 0] = jnp.broadcast_to(pe[ci], (8, 2 * n))
    lv = [_dot(lm[ci], v[ci]) for ci in chunks]
    for ci in chunks:
        y0_ref[ci, 0] = lv[ci][2 * c:]
        g_ref[ci, 0] = _tdot(v[ci], kt[ci] * pe[ci])
    x = [jnp.concatenate([at[ci], lv[ci][:2 * c]], axis=1) for ci in chunks]
    x = [x[ci] + _dot(l_pow[ci], x[ci]) for ci in chunks]
    span = 1
    while span * 2 < c:
        l_pow = [_dot(z, z) for z in l_pow]
        x = [x[ci] + _dot(l_pow[ci], x[ci]) for ci in chunks]
        span *= 2
    for ci in chunks:
        wr_ref[ci, 0, :2 * c, :] = x[ci][:, :2 * n].astype(BF16)
        u_ref[ci, 0] = x[ci][:, 2 * n:]


def _wkv_chunk_fwd(s):
    return (s + SEQ // WKV_CHUNK) % WKV_CHUNKS


def _wkv_chunk_bwd(s):
    return WKV_CHUNKS - 1 - s


def wkv_prep(r, lw, k, v, a, b, reverse):
    c, n2 = WKV_CHUNK, 2 * RWKV_HEAD_DIM
    cb = WKV_PREP_CHUNKS
    spec = pl.BlockSpec((cb * c, n2), lambda p, i: (i, p))

    def out(rows, dtype):
        return (pl.BlockSpec((cb, 1, rows, n2), lambda p, i: (i, p, 0, 0)),
                jax.ShapeDtypeStruct((WKV_CHUNKS, WKV_PAIRS, rows, n2), dtype))

    outs = [out(4 * c, BF16), out(2 * c, F32), out(2 * c, F32), out(2 * c, BF16), out(2 * c, BF16),
            out(2 * c, F32), out(8, F32)]
    return pl.pallas_call(
        functools.partial(_wkv_prep_kernel, reverse=reverse),
        grid=(WKV_PAIRS, WKV_CHUNKS // cb),
        in_specs=[spec] * 6,
        out_specs=[o[0] for o in outs],
        out_shape=[o[1] for o in outs],
        compiler_params=_params("parallel", "parallel"),
        name="wkv_prep_bwd" if reverse else "wkv_prep_fwd",
    )(r, lw, k, v, a, b)


def _wkv_scan_kernel(*refs):
    c, n = WKV_CHUNK, RWKV_HEAD_DIM
    ins, (yf_ref, yb_ref, s_ref) = refs[:14], refs[14:]

    @pl.when(pl.program_id(0) == 0)
    def _():
        s_ref[...] = jnp.zeros_like(s_ref)

    jobs = [(d, p) for d in range(2) for p in range(WKV_PAIRS)]
    ref = lambda d, k: ins[7 * d + k]
    s0 = [s_ref[d, p] for d, p in jobs]
    z = [_dot_t(ref(d, 0)[0, p], s0[j]) for j, (d, p) in enumerate(jobs)]
    sa = [z[j][:2 * c] + ref(d, 1)[0, p] for j, (d, p) in enumerate(jobs)]
    for j, (d, p) in enumerate(jobs):
        s_ref[d, p] = s0[j] * ref(d, 6)[0, p, 0:1, :] + ref(d, 5)[0, p] + _tdot(sa[j], ref(d, 4)[0, p])
    for j, (d, p) in enumerate(jobs):
        y = z[j][2 * c:] + ref(d, 2)[0, p] + _dot(ref(d, 3)[0, p], sa[j])
        (yf_ref, yb_ref)[d][:, p * 2 * n:(p + 1) * 2 * n] = y[:c] + y[c:]


def wkv_scan(prep_f, prep_b):
    c, n2 = WKV_CHUNK, 2 * RWKV_HEAD_DIM

    def specs(chunk_of):
        return [pl.BlockSpec((1, WKV_PAIRS) + a.shape[2:], lambda s: (chunk_of(s), 0, 0, 0)) for a in prep_f]

    y_shape = jax.ShapeDtypeStruct((N_TOK, RWKV_WIDTH), F32)
    return pl.pallas_call(
        _wkv_scan_kernel,
        grid=(WKV_CHUNKS,),
        in_specs=specs(_wkv_chunk_fwd) + specs(_wkv_chunk_bwd),
        out_specs=[pl.BlockSpec((c, RWKV_WIDTH), lambda s: (_wkv_chunk_fwd(s), 0)),
                   pl.BlockSpec((c, RWKV_WIDTH), lambda s: (_wkv_chunk_bwd(s), 0))],
        out_shape=[y_shape, y_shape],
        scratch_shapes=[pltpu.VMEM((2, WKV_PAIRS, n2, n2), F32)],
        compiler_params=_params("arbitrary"),
        name="wkv_scan",
    )(*prep_f, *prep_b)


MOE_CAST_ROWS = 256


def _moe_kernel(be_ref, idx_ref, idx_next_ref, tok_hbm, w1_ref, b1_ref, w2_ref, b2_ref, g_ref, o_ref,
                w1b_ref, w2b_ref, xbuf, sem):
    i = pl.program_id(0)
    cur = i % 2

    def row_copy(idx, r, buf):
        return pltpu.make_async_copy(tok_hbm.at[idx[0, 0, r]], xbuf.at[buf, r], sem.at[buf])

    def start_rows(idx, buf):
        def body(r, carry):
            row_copy(idx, r, buf).start()
            return carry
        lax.fori_loop(0, DISPATCH_BLOCK, body, 0, unroll=8)

    @pl.when(i == 0)
    def _():
        start_rows(idx_ref, 0)

    @pl.when(i + 1 < pl.num_programs(0))
    def _():
        start_rows(idx_next_ref, 1 - cur)

    new_expert = (i == 0) | (be_ref[i] != be_ref[jnp.maximum(i - 1, 0)])

    @pl.when(new_expert)
    def _():
        def cast1(c, carry):
            rows = pl.ds(pl.multiple_of(c * MOE_CAST_ROWS, MOE_CAST_ROWS), MOE_CAST_ROWS)
            w1b_ref[rows, :] = w1_ref[0, rows, :].astype(BF16)
            return carry

        def cast2(c, carry):
            rows = pl.ds(pl.multiple_of(c * MOE_CAST_ROWS, MOE_CAST_ROWS), MOE_CAST_ROWS)
            w2b_ref[rows, :] = w2_ref[0, rows, :].astype(BF16)
            return carry

        lax.fori_loop(0, D_MODEL // MOE_CAST_ROWS, cast1, 0)
        lax.fori_loop(0, EXPERT_FF // MOE_CAST_ROWS, cast2, 0)

    def wait_row(r, carry):
        row_copy(idx_ref, r, cur).wait()
        return carry
    lax.fori_loop(0, DISPATCH_BLOCK, wait_row, 0, unroll=8)

    sw = TOKEN_SLAB[1]
    u = b1_ref[0]
    for s in range(TOKEN_SLAB[0]):
        u = u + jnp.dot(xbuf[cur, :, s, :], w1b_ref[s * sw:(s + 1) * sw, :], preferred_element_type=F32)
    u_glu = jnp.minimum(u[:, :EXPERT_FF], SWIGLU_LIMIT)
    u_lin = jnp.clip(u[:, EXPERT_FF:], -SWIGLU_LIMIT, SWIGLU_LIMIT)
    act = (u_glu * jax.nn.sigmoid(SWIGLU_ALPHA * u_glu) * (u_lin + 1.0)).astype(BF16)
    gate = g_ref[...]
    ew = EXPERT_SLAB[1]
    for s in range(EXPERT_SLAB[0]):
        cols = slice(s * ew, (s + 1) * ew)
        y = jnp.dot(act, w2b_ref[:, cols], preferred_element_type=F32) + b2_ref[0, :, cols]
        o_ref[:, s, :] = y * gate


def moe_experts(tokens_bf16, slot_token, slot_gate, block_expert, w1, b1, w2, b2, layer):
    n_slots = slot_token.shape[0]
    n_blocks = n_slots // DISPATCH_BLOCK
    idx = slot_token.reshape(n_blocks, 1, DISPATCH_BLOCK)
    idx_spec = lambda nxt: pl.BlockSpec((1, 1, DISPATCH_BLOCK),
                                        lambda i, be: (jnp.minimum(i + nxt, n_blocks - 1), 0, 0),
                                        memory_space=pltpu.SMEM)
    grid_spec = pltpu.PrefetchScalarGridSpec(
        num_scalar_prefetch=1,
        grid=(n_blocks,),
        in_specs=[idx_spec(0), idx_spec(1),
                  pl.BlockSpec(memory_space=pl.ANY),
                  pl.BlockSpec((None, 1, D_MODEL, 2 * EXPERT_FF), lambda i, be: (layer, be[i], 0, 0),
                               pipeline_mode=pl.Buffered(1)),
                  pl.BlockSpec((1, 1, 2 * EXPERT_FF), lambda i, be: (be[i], 0, 0)),
                  pl.BlockSpec((None, 1, EXPERT_FF, D_MODEL), lambda i, be: (layer, be[i], 0, 0),
                               pipeline_mode=pl.Buffered(1)),
                  pl.BlockSpec((1, 1, D_MODEL), lambda i, be: (be[i], 0, 0)),
                  pl.BlockSpec((DISPATCH_BLOCK, 1), lambda i, be: (i, 0))],
        out_specs=pl.BlockSpec((DISPATCH_BLOCK,) + EXPERT_SLAB, lambda i, be: (i, 0, 0)),
        scratch_shapes=[pltpu.VMEM((D_MODEL, 2 * EXPERT_FF), BF16), pltpu.VMEM((EXPERT_FF, D_MODEL), BF16),
                        pltpu.VMEM((2, DISPATCH_BLOCK) + TOKEN_SLAB, BF16), pltpu.SemaphoreType.DMA((2,))],
    )
    return pl.pallas_call(
        _moe_kernel,
        grid_spec=grid_spec,
        out_shape=jax.ShapeDtypeStruct((n_slots,) + EXPERT_SLAB, F32),
        compiler_params=_params("arbitrary", vmem=MOE_VMEM_LIMIT),
        name="moe_experts",
    )(block_expert, idx, idx, tokens_bf16, w1, b1.reshape(N_EXPERTS, 1, -1), w2, b2.reshape(N_EXPERTS, 1, -1),
      slot_gate.reshape(n_slots, 1))


COMBINE_ROWS = 64


def _combine_ln_kernel(idx_ref, idx_next_ref, y_hbm, x_ref, g_ref, lng_ref, lnb_ref, o_ref, buf, sem):
    i = pl.program_id(0)
    cur = i % 2

    def row_copy(idx, r, j, b):
        return pltpu.make_async_copy(y_hbm.at[idx[0, 0, r * TOP_K + j]], buf.at[b, j, r], sem.at[b])

    def start_rows(idx, b):
        def body(r, carry):
            for j in range(TOP_K):
                row_copy(idx, r, j, b).start()
            return carry
        lax.fori_loop(0, COMBINE_ROWS, body, 0, unroll=4)

    @pl.when(i == 0)
    def _():
        start_rows(idx_ref, 0)

    @pl.when(i + 1 < pl.num_programs(0))
    def _():
        start_rows(idx_next_ref, 1 - cur)

    def wait_rows(r, carry):
        for j in range(TOP_K):
            row_copy(idx_ref, r, j, cur).wait()
        return carry
    lax.fori_loop(0, COMBINE_ROWS, wait_rows, 0, unroll=4)

    parts = []
    for s in range(EXPERT_SLAB[0]):
        part = buf[cur, 0, :, s, :]
        for j in range(1, TOP_K):
            part = part + buf[cur, j, :, s, :]
        parts.append(part)
    f = jnp.concatenate(parts, axis=1)
    o_ref[...] = _layernorm(DEEPNORM_ALPHA * x_ref[...] + g_ref[0] * f, lng_ref[...], lnb_ref[...])


def moe_combine_ln(y, slot, x, g, ln_g, ln_b):
    n = x.shape[0]
    nb = n // COMBINE_ROWS
    idx = slot.reshape(nb, 1, COMBINE_ROWS * TOP_K)
    idx_spec = lambda nxt: pl.BlockSpec((1, 1, COMBINE_ROWS * TOP_K),
                                        lambda i: (jnp.minimum(i + nxt, nb - 1), 0, 0),
                                        memory_space=pltpu.SMEM)
    row = pl.BlockSpec((COMBINE_ROWS, D_MODEL), lambda i: (i, 0))
    mod = pl.BlockSpec((1, 1, D_MODEL), lambda i: (i // (SEQ // COMBINE_ROWS), 0, 0))
    vec = pl.BlockSpec((1, D_MODEL), lambda i: (0, 0))
    return pl.pallas_call(
        _combine_ln_kernel,
        grid=(nb,),
        in_specs=[idx_spec(0), idx_spec(1), pl.BlockSpec(memory_space=pl.ANY), row, mod, vec, vec],
        out_specs=row,
        out_shape=jax.ShapeDtypeStruct((n, D_MODEL), F32),
        scratch_shapes=[pltpu.VMEM((2, TOP_K, COMBINE_ROWS) + EXPERT_SLAB, F32), pltpu.SemaphoreType.DMA((2,))],
        compiler_params=_params("arbitrary"),
        name="moe_combine_ln",
    )(idx, idx, y, x, g, ln_g.reshape(1, D_MODEL), ln_b.reshape(1, D_MODEL))


def moe(tokens, tokens_bf16, router_w, router_b, w1, b1, w2, b2, layer):
    n_tok = tokens.shape[0]
    n_assign = n_tok * TOP_K
    n_blocks = -(-(n_assign + N_EXPERTS * (DISPATCH_BLOCK - 1)) // DISPATCH_BLOCK)
    n_slots = n_blocks * DISPATCH_BLOCK
    logits = router_logits(tokens, router_w, router_b)
    top_val, top_idx = lax.top_k(logits, TOP_K)
    gate = jax.nn.softmax(top_val, axis=-1).reshape(-1)
    expert = top_idx.reshape(-1)
    onehot = (expert[:, None] == jnp.arange(N_EXPERTS, dtype=expert.dtype)[None, :]).astype(jnp.int32)
    before = jnp.cumsum(onehot, axis=0) - onehot
    counts = before[-1] + onehot[-1]
    padded = (counts + DISPATCH_BLOCK - 1) // DISPATCH_BLOCK * DISPATCH_BLOCK
    pad_end = jnp.cumsum(padded)
    pad_start = pad_end - padded
    slot = jnp.sum(onehot * (before + pad_start[None, :]), axis=1).astype(jnp.int32)
    slot_assign = jnp.full((n_slots,), -1, jnp.int32).at[slot].set(jnp.arange(n_assign, dtype=jnp.int32))
    filled = slot_assign >= 0
    src = jnp.maximum(slot_assign, 0)
    slot_token = jnp.where(filled, src // TOP_K, 0)
    slot_gate = jnp.where(filled, gate[src], 0.0)
    block_start = jnp.arange(n_blocks, dtype=pad_end.dtype) * DISPATCH_BLOCK
    block_expert = jnp.minimum(jnp.sum(pad_end[None, :] <= block_start[:, None], axis=1),
                               N_EXPERTS - 1).astype(jnp.int32)
    y = moe_experts(tokens_bf16, slot_token, slot_gate, block_expert, w1, b1, w2, b2, layer)
    return y, slot.reshape(n_tok, TOP_K)


def _rope_lanes(x, cos, sin, half):
    lane = lax.broadcasted_iota(jnp.int32, x.shape, 1)
    up = pltpu.roll(x, 128 - half, 1)
    down = pltpu.roll(x, half, 1)
    return x * cos + jnp.where(lane % (2 * half) < half, up, down) * sin


def _rms(x, g):
    return x * lax.rsqrt(jnp.mean(x * x, axis=-1, keepdims=True) + RMS_EPS) * g


def _attn_prep_kernel(pa_ref, pb_ref, pc_ref, gq_ref, gk_ref, gmq_ref, gmkv_ref, cosa_ref, sina_ref,
                      cosb_ref, sinb_ref, wuq_ref, wukv_ref, qg_ref, kg_ref, vg_ref, qc_ref, kc_ref, vm_ref):
    hd = GQA_HEAD_DIM
    ca, sa, cb, sb = cosa_ref[...], sina_ref[...], cosb_ref[...], sinb_ref[...]
    grp = GQA_HEADS // GQA_KV_HEADS
    for h in range(GQA_HEADS):
        q = _rope_lanes(_rms(pa_ref[:, h * hd:(h + 1) * hd], gq_ref[...]), ca, sa, hd // 4)
        qg_ref[h // grp, h % grp] = (q * GQA_HEAD_DIM ** -0.5).astype(BF16)
    for h in range(GQA_KV_HEADS):
        k = _rope_lanes(_rms(pb_ref[:, h * hd:(h + 1) * hd], gk_ref[...]), ca, sa, hd // 4)
        kg_ref[h] = k.astype(BF16)
        vg_ref[h] = pb_ref[:, (GQA_KV_HEADS + h) * hd:(GQA_KV_HEADS + h + 1) * hd].astype(BF16)
    dkvn = _rms(pb_ref[:, 2 * GQA_KV_HEADS * hd:], gmkv_ref[...])
    dqn = _rms(pc_ref[:, :MLA_Q_RANK], gmq_ref[...])
    k_rope = _rope_lanes(pc_ref[:, MLA_Q_RANK:], cb, sb, MLA_ROPE_DIM // 4)
    qb = _dot(dqn, wuq_ref[...])
    kvb = _dot(dkvn, wukv_ref[...])
    scale = MLA_QK_DIM ** -0.5
    for h in range(MLA_HEADS):
        base = h * MLA_CAT_DIM
        q_nope = qb[:, base:base + 128]
        q_rope = _rope_lanes(qb[:, base + 128:base + 256], cb, sb, MLA_ROPE_DIM // 4)
        qc_ref[h, 0] = (jnp.concatenate([q_nope, q_rope], axis=1) * scale).astype(BF16)
        kc_ref[h] = jnp.concatenate([kvb[:, base:base + 128], k_rope], axis=1).astype(BF16)
        vm_ref[h] = kvb[:, base + 128:base + 256].astype(BF16)


def _rope_lane_tables(rot_dim):
    n_freq = rot_dim // 4
    row = jnp.repeat(jnp.arange(SEQ // GRID_W, dtype=F32), GRID_W)
    col = (jnp.arange(SEQ) % GRID_W).astype(F32)
    freqs = ROPE_THETA ** (-jnp.arange(n_freq, dtype=F32) / n_freq)
    ar, ac = row[:, None] * freqs, col[:, None] * freqs
    cos = jnp.concatenate([jnp.cos(ar), jnp.cos(ar), jnp.cos(ac), jnp.cos(ac)], axis=1)
    sin = jnp.concatenate([-jnp.sin(ar), jnp.sin(ar), -jnp.sin(ac), jnp.sin(ac)], axis=1)
    cos = jnp.pad(cos, ((0, CTX_LEN), (0, 128 - rot_dim)), constant_values=1.0)
    sin = jnp.pad(sin, ((0, CTX_LEN), (0, 128 - rot_dim)))
    return cos, sin


def attn_prep(proj, gqa_q_norm, gqa_k_norm, mla_q_norm, mla_kv_norm, mla_w_uq, mla_w_ukv):
    t = N_TOK
    cos_a, sin_a = _rope_lane_tables(GQA_HEAD_DIM)
    cos_b, sin_b = _rope_lane_tables(MLA_ROPE_DIM)
    w_uq = jnp.pad(mla_w_uq.reshape(MLA_Q_RANK, MLA_HEADS, MLA_QK_DIM),
                   ((0, 0), (0, 0), (0, MLA_CAT_DIM - MLA_QK_DIM))).reshape(MLA_Q_RANK, -1).astype(BF16)
    w_ukv = mla_w_ukv.astype(BF16)
    first = SMALL_COL0 // 1024
    col = lambda j: pl.BlockSpec((ROW_TILE, 1024), lambda i: (i, first + j))
    vec = lambda n: pl.BlockSpec((1, n), lambda i: (0, 0))
    tab = pl.BlockSpec((ROW_TILE, 128), lambda i: (i, 0))
    full = lambda a: pl.BlockSpec(a.shape, lambda i: (0, 0))
    grp = GQA_HEADS // GQA_KV_HEADS
    hd = GQA_HEAD_DIM
    outs = [
        (pl.BlockSpec((GQA_KV_HEADS, grp, ROW_TILE, hd), lambda i: (0, 0, i, 0)), (GQA_KV_HEADS, grp, t, hd)),
        (pl.BlockSpec((GQA_KV_HEADS, ROW_TILE, hd), lambda i: (0, i, 0)), (GQA_KV_HEADS, t, hd)),
        (pl.BlockSpec((GQA_KV_HEADS, ROW_TILE, hd), lambda i: (0, i, 0)), (GQA_KV_HEADS, t, hd)),
        (pl.BlockSpec((MLA_HEADS, 1, ROW_TILE, MLA_CAT_DIM), lambda i: (0, 0, i, 0)), (MLA_HEADS, 1, t, MLA_CAT_DIM)),
        (pl.BlockSpec((MLA_HEADS, ROW_TILE, MLA_CAT_DIM), lambda i: (0, i, 0)), (MLA_HEADS, t, MLA_CAT_DIM)),
        (pl.BlockSpec((MLA_HEADS, ROW_TILE, MLA_V_DIM), lambda i: (0, i, 0)), (MLA_HEADS, t, MLA_V_DIM)),
    ]
    return pl.pallas_call(
        _attn_prep_kernel,
        grid=(t // ROW_TILE,),
        in_specs=[col(0), col(1), col(2), vec(hd), vec(hd), vec(MLA_Q_RANK), vec(MLA_KV_RANK),
                  tab, tab, tab, tab, full(w_uq), full(w_ukv)],
        out_specs=[o[0] for o in outs],
        out_shape=[jax.ShapeDtypeStruct(o[1], BF16) for o in outs],
        compiler_params=_params("parallel"),
        name="attn_prep",
    )(proj, proj, proj, gqa_q_norm.reshape(1, -1), gqa_k_norm.reshape(1, -1), mla_q_norm.reshape(1, -1),
      mla_kv_norm.reshape(1, -1), cos_a, sin_a, cos_b, sin_b, w_uq, w_ukv)


def _group_sum(x, ones_bd):
    hi = x.astype(BF16)
    lo = (x - hi.astype(F32)).astype(BF16)
    return jnp.dot(hi, ones_bd, preferred_element_type=F32) + jnp.dot(lo, ones_bd, preferred_element_type=F32)


def _softplus(z):
    return jnp.maximum(z, 0.0) + jnp.log(1.0 + jnp.exp(-jnp.abs(z)))


def _rwkv_prep_kernel(p_ref, prev_ref, next_ref, mu_ref, wup_ref, w0_ref, aup_ref, a0_ref, gup_ref,
                      kk_ref, ka_ref, rk_ref, ones_ref,
                      r_ref, v_ref, a_ref, lwf_ref, kdf_ref, bf_ref, lwb_ref, kdb_ref, bb_ref, g_ref, bonus_ref):
    i = pl.program_id(0)
    w = RWKV_WIDTH
    p = p_ref[...]
    n = p.shape[0]
    rowid = lax.broadcasted_iota(jnp.int32, (n, 1), 0)
    has_prev = jnp.where(i % LAT_TILES == 0, 0.0, 1.0)
    has_next = jnp.where((i == LAT_TILES - 1) | (i == LAT_TILES), 0.0, 1.0)
    prev = jnp.where(rowid == 0, prev_ref[7:8, :] * has_prev, pltpu.roll(p, 1, 0))
    nxt = jnp.where(rowid == n - 1, next_ref[0:1, :] * has_next, pltpu.roll(p, n - 1, 0))
    p = p + mu_ref[0:1, :] * (prev - p) + mu_ref[1:2, :] * (nxt - p)

    ones_bd = ones_ref[...]
    r, kx, vx = p[:, :w], p[:, w:2 * w], p[:, 2 * w:3 * w]
    low = p[:, 3 * w:]
    g_ref[...] = _dot(jax.nn.sigmoid(low[:, 512:]), gup_ref[...])
    kk = kx * kk_ref[...]
    kk = kk / jnp.maximum(jnp.sqrt(_group_sum(kk * kk, ones_bd)), 1e-12)
    r_ref[...] = r
    v_ref[...] = vx
    a_ref[...] = -kk
    k_sum = jnp.zeros_like(kx)
    for d, (lw_ref, kd_ref, b_ref) in enumerate(((lwf_ref, kdf_ref, bf_ref), (lwb_ref, kdb_ref, bb_ref))):
        wd = low[:, d * 128:(d + 1) * 128]
        ad = low[:, 256 + d * 128:256 + (d + 1) * 128]
        w_lin = _dot(jnp.tanh(wd), wup_ref[d]) + w0_ref[d:d + 1, :]
        lw_ref[...] = -jnp.exp(-_softplus(-w_lin) - 0.5)
        a_gate = jax.nn.sigmoid(_dot(ad, aup_ref[d]) + a0_ref[d:d + 1, :])
        k_d = kx * (1.0 + (a_gate - 1.0) * ka_ref[...])
        kd_ref[...] = k_d
        b_ref[...] = kk * a_gate
        k_sum = k_sum + k_d
    bonus_ref[...] = _group_sum(r * k_sum * rk_ref[...], ones_bd) * vx


def _head_ones():
    h = jnp.arange(RWKV_WIDTH) // RWKV_HEAD_DIM
    return (h[:, None] == h[None, :]).astype(BF16)


def rwkv_prep(proj, rwkv_mu, rwkv_w0, rwkv_w_up, rwkv_a0, rwkv_a_up, rwkv_g_up, rwkv_k_k, rwkv_k_a, rwkv_r_k):
    t, w = N_TOK, RWKV_WIDTH
    cb = RWKV_COL0 // RWKV_IN_PAD
    n8 = t // 8
    per8 = ROW_TILE // 8
    mu = jnp.pad(rwkv_mu, ((0, 6), (0, RWKV_IN_PAD - RWKV_IN)))
    g_up = jnp.pad(rwkv_g_up, ((0, 512 - RWKV_GATE_RANK), (0, 0))).astype(BF16)
    full = lambda a: pl.BlockSpec(a.shape, lambda i: (0,) * a.ndim)
    consts = [mu, rwkv_w_up.astype(BF16), rwkv_w0, rwkv_a_up.astype(BF16), rwkv_a0, g_up,
              rwkv_k_k.reshape(1, w), rwkv_k_a.reshape(1, w), rwkv_r_k.reshape(1, w), _head_ones()]
    out_spec = pl.BlockSpec((ROW_TILE, w), lambda i: (i, 0))
    return pl.pallas_call(
        _rwkv_prep_kernel,
        grid=(t // ROW_TILE,),
        in_specs=[pl.BlockSpec((ROW_TILE, RWKV_IN_PAD), lambda i: (i, cb)),
                  pl.BlockSpec((8, RWKV_IN_PAD), lambda i: (jnp.maximum(i * per8 - 1, 0), cb)),
                  pl.BlockSpec((8, RWKV_IN_PAD), lambda i: (jnp.minimum((i + 1) * per8, n8 - 1), cb))]
                 + [full(a) for a in consts],
        out_specs=[out_spec] * 11,
        out_shape=[jax.ShapeDtypeStruct((t, w), F32)] * 11,
        compiler_params=_params("parallel"),
        name="rwkv_prep",
    )(proj, proj, proj, *consts)


def _rwkv_out_kernel(yf_ref, yb_ref, bonus_ref, g_ref, gng_ref, gnb_ref, ones_ref, o_ref):
    ones_bd = ones_ref[...]
    y = yf_ref[...] + yb_ref[...]
    yc = y - _group_sum(y, ones_bd) * (1.0 / RWKV_HEAD_DIM)
    var = _group_sum(yc * yc, ones_bd) * (1.0 / RWKV_HEAD_DIM)
    yn = yc * lax.rsqrt(var + GN_EPS) * gng_ref[...] + gnb_ref[...]
    o_ref[...] = ((yn + bonus_ref[...]) * g_ref[...]).astype(o_ref.dtype)


def rwkv_out(y_f, y_b, bonus, g, gn_g, gn_b):
    t, w = N_TOK, RWKV_WIDTH
    row = pl.BlockSpec((ROW_TILE, w), lambda i: (i, 0))
    vec = pl.BlockSpec((1, w), lambda i: (0, 0))
    ones = _head_ones()
    return pl.pallas_call(
        _rwkv_out_kernel,
        grid=(t // ROW_TILE,),
        in_specs=[row, row, row, row, vec, vec, pl.BlockSpec(ones.shape, lambda i: (0, 0))],
        out_specs=row,
        out_shape=jax.ShapeDtypeStruct((t, w), BF16),
        compiler_params=_params("parallel"),
        name="rwkv_out",
    )(y_f, y_b, bonus, g, gn_g.reshape(1, w), gn_b.reshape(1, w), ones)


def _merge_kernel(oa_ref, ob_ref, oc_ref, wa_ref, wb_ref, wc_ref, ga_ref, gb_ref, gc_ref, m_ref):
    acc = jax.nn.sigmoid(ga_ref[...]) * _dot(oa_ref[...], wa_ref[0])
    acc += jax.nn.sigmoid(gb_ref[...]) * _dot(ob_ref[...], wb_ref[0])
    acc += jax.nn.sigmoid(gc_ref[...]) * _dot(oc_ref[...], wc_ref[0])
    m_ref[...] = acc.astype(m_ref.dtype)


def merge(oa, ob, oc, proj, w_branch_up):
    t = N_TOK
    tm, tn = 768, 512
    nj = D_MODEL // tn
    wb = w_branch_up.astype(BF16)
    o_spec = pl.BlockSpec((tm, BRANCH_WIDTH), lambda i, j: (i, 0))
    w_spec = lambda b: pl.BlockSpec((1, BRANCH_WIDTH, tn), lambda i, j: (b, 0, j))
    g_spec = lambda b: pl.BlockSpec((tm, tn), lambda i, j: (i, b * nj + j))
    return pl.pallas_call(
        _merge_kernel,
        grid=(t // tm, nj),
        in_specs=[o_spec, o_spec, o_spec, w_spec(0), w_spec(1), w_spec(2), g_spec(0), g_spec(1), g_spec(2)],
        out_specs=pl.BlockSpec((tm, tn), lambda i, j: (i, j)),
        out_shape=jax.ShapeDtypeStruct((t, D_MODEL), BF16),
        compiler_params=_params("parallel", "arbitrary"),
        name="merge",
    )(oa, ob, oc, wb, wb, wb, proj, proj, proj)


def _split(x, sizes):
    return jnp.split(x, np.cumsum(sizes)[:-1].tolist(), axis=-1)


def _in_proj_weight(w_in):
    aq, ak, av, dq, dkv, kr, rwkv, gates = _split(w_in, IN_SPLITS)
    small = jnp.concatenate([aq, ak, av, dkv, dq, kr], axis=1)
    pad = lambda z, n: jnp.pad(z, ((0, 0), (0, n - z.shape[1])))
    return jnp.concatenate([gates, pad(rwkv, RWKV_IN_PAD), pad(small, SMALL_WIDTH)], axis=1).astype(BF16)


def _mixer(h, w_in, gqa_q_norm, gqa_k_norm, mla_q_norm, mla_kv_norm, mla_w_uq, mla_w_ukv,
           rwkv_mu, rwkv_w0, rwkv_w_up, rwkv_a0, rwkv_a_up, rwkv_g_up, rwkv_k_k, rwkv_k_a, rwkv_r_k,
           rwkv_gn_g, rwkv_gn_b, w_branch_up, w_o):
    proj = matmul(h, _in_proj_weight(w_in), name="in_proj")

    qg, kg, vg, qc, kc, vm = attn_prep(proj, gqa_q_norm, gqa_k_norm, mla_q_norm, mla_kv_norm, mla_w_uq, mla_w_ukv)
    oa = lat_ctx_attention(qg, kg, vg, 128, "gqa")
    ob = lat_ctx_attention(qc, kc, vm, 512, "mla")

    r, v, a, lw_f, kd_f, b_f, lw_b, kd_b, b_b, g, bonus = rwkv_prep(
        proj, rwkv_mu, rwkv_w0, rwkv_w_up, rwkv_a0, rwkv_a_up, rwkv_g_up, rwkv_k_k, rwkv_k_a, rwkv_r_k)
    y_f, y_b = wkv_scan(wkv_prep(r, lw_f, kd_f, v, a, b_f, reverse=False),
                        wkv_prep(r, lw_b, kd_b, v, a, b_b, reverse=True))
    oc = rwkv_out(y_f, y_b, bonus, g, rwkv_gn_g, rwkv_gn_b)

    m = merge(oa, ob, oc, proj, w_branch_up)
    return matmul(m, w_o.astype(BF16), name="out_proj")


def kernel(x, c, ctx, c_ctx, w_ada, b_ada, w_in, gqa_q_norm, gqa_k_norm, mla_q_norm, mla_kv_norm, mla_w_uq,
           mla_w_ukv, rwkv_mu, rwkv_w0, rwkv_w_up, rwkv_a0, rwkv_a_up, rwkv_g_up, rwkv_k_k, rwkv_k_a, rwkv_r_k,
           rwkv_gn_g, rwkv_gn_b, w_branch_up, w_o, ln1_g, ln1_b, router_w, router_b, exp_w_in, exp_b_in,
           exp_w_out, exp_b_out, ln2_g, ln2_b):
    xs = jnp.concatenate([x[0], ctx[0]], axis=0)
    cond = jnp.zeros((16, D_MODEL), F32).at[0].set(jax.nn.silu(c[0])).at[1].set(jax.nn.silu(c_ctx))
    for l in range(DEPTH):
        ada = matmul(cond.astype(BF16), w_ada, bias=b_ada[l], tm=16, tn=512, layer=l, name="adaln")
        sh1, sc1, g1, sh2, sc2, g2 = (z[:2].reshape(2, 1, D_MODEL) for z in _split(ada, (D_MODEL,) * 6))
        h = modulate(xs, sc1, sh1)
        m = _mixer(h, w_in[l], gqa_q_norm[l], gqa_k_norm[l], mla_q_norm[l], mla_kv_norm[l], mla_w_uq[l],
                   mla_w_ukv[l], rwkv_mu[l], rwkv_w0[l], rwkv_w_up[l], rwkv_a0[l], rwkv_a_up[l], rwkv_g_up[l],
                   rwkv_k_k[l], rwkv_k_a[l], rwkv_r_k[l], rwkv_gn_g[l], rwkv_gn_b[l], w_branch_up[l], w_o[l])
        xs, tokens, tokens_bf16 = post_ln(xs, m, g1, ln1_g[l], ln1_b[l], sc2, sh2)
        y, slot = moe(tokens, tokens_bf16, router_w[l], router_b[l], exp_w_in, exp_b_in[l], exp_w_out,
                      exp_b_out[l], l)
        xs = moe_combine_ln(y, slot, xs, g2, ln2_g[l], ln2_b[l])
    return xs[:SEQ].reshape(1, SEQ, D_MODEL)
```

```python
import functools

import jax
import jax.numpy as jnp
import numpy as np
from jax import lax
from jax.experimental import pallas as pl
from jax.experimental.pallas import tpu as pltpu

F32, BF16 = jnp.float32, jnp.bfloat16

D_MODEL = 4096
SEQ = 8192
CTX_LEN = 256
N_TOK = SEQ + CTX_LEN
DEPTH = 2
GRID_W = 64
ROPE_THETA = 10000.0

GQA_HEADS, GQA_KV_HEADS, GQA_HEAD_DIM = 8, 2, 128
MLA_HEADS, MLA_Q_RANK, MLA_KV_RANK = 8, 896, 512
MLA_NOPE_DIM, MLA_ROPE_DIM, MLA_V_DIM = 128, 64, 128
MLA_QK_DIM = MLA_NOPE_DIM + MLA_ROPE_DIM
MLA_CAT_DIM = 256
RWKV_HEADS, RWKV_HEAD_DIM = 16, 64
RWKV_WIDTH = RWKV_HEADS * RWKV_HEAD_DIM
RWKV_DECAY_RANK, RWKV_ICLR_RANK, RWKV_GATE_RANK = 128, 128, 480
RWKV_SPLITS = (RWKV_WIDTH, RWKV_WIDTH, RWKV_WIDTH, RWKV_DECAY_RANK, RWKV_DECAY_RANK,
               RWKV_ICLR_RANK, RWKV_ICLR_RANK, RWKV_GATE_RANK)
RWKV_IN = sum(RWKV_SPLITS)
RWKV_IN_PAD = 4096
BRANCH_WIDTH = 1024
N_BRANCHES = 3
N_EXPERTS, TOP_K, EXPERT_FF = 32, 4, 512
SWIGLU_ALPHA, SWIGLU_LIMIT = 1.702, 7.0
DISPATCH_BLOCK = 128
DEEPNORM_ALPHA = (2 * DEPTH) ** 0.25
LN_EPS, RMS_EPS, GN_EPS = 1e-5, 1e-6, 64e-5

IN_SPLITS = (1024, 256, 256, MLA_Q_RANK, MLA_KV_RANK, MLA_ROPE_DIM, RWKV_IN, N_BRANCHES * D_MODEL)
SMALL_SPLITS = IN_SPLITS[:6]
SMALL_WIDTH = 3072
RWKV_COL0 = N_BRANCHES * D_MODEL
SMALL_COL0 = RWKV_COL0 + RWKV_IN_PAD

ROW_TILE = 256
LAT_TILES = SEQ // ROW_TILE
WKV_CHUNK = 64
WKV_PAIRS = RWKV_HEADS // 2
WKV_CHUNKS = N_TOK // WKV_CHUNK
WKV_CTX_CHUNKS = CTX_LEN // WKV_CHUNK
WKV_PREP_CHUNKS = 12
ATTN_KV_TILE = 768
TOKEN_SLAB = (8, 512)
EXPERT_SLAB = (8, 512)
VMEM_LIMIT = 48 * 1024 * 1024
MOE_VMEM_LIMIT = 56 * 1024 * 1024


def _params(*sem, vmem=VMEM_LIMIT):
    return pltpu.CompilerParams(dimension_semantics=sem, vmem_limit_bytes=vmem)


def _dot(a, b):
    return jnp.dot(a.astype(BF16), b.astype(BF16), preferred_element_type=F32)


def _dot_t(a, b):
    return lax.dot_general(a.astype(BF16), b.astype(BF16), (((1,), (1,)), ((), ())), preferred_element_type=F32)


def _tdot(a, b):
    return lax.dot_general(a.astype(BF16), b.astype(BF16), (((0,), (0,)), ((), ())), preferred_element_type=F32)


def _mm_kernel(a_ref, b_ref, bias_ref, o_ref):
    o_ref[...] = (_dot(a_ref[...], b_ref[...]) + bias_ref[...]).astype(o_ref.dtype)


def _pick(n, prefs):
    for p in prefs:
        if n % p == 0:
            return p
    return n


def matmul(a, b, bias=None, out_dtype=F32, tm=None, tn=None, layer=None, name="matmul"):
    m, kdim = a.shape
    n = b.shape[-1]
    tm = tm or _pick(m, (768, 512, 256, 128, 16))
    tn = tn or _pick(n, (512, 256, 128))
    if bias is None:
        bias = jnp.zeros((n,), F32)
    bias = bias.reshape(1, n).astype(F32)
    if layer is None:
        b_spec = pl.BlockSpec((kdim, tn), lambda i, j: (0, j))
    else:
        b_spec = pl.BlockSpec((None, kdim, tn), lambda i, j: (layer, 0, j))
    return pl.pallas_call(
        _mm_kernel,
        grid=(m // tm, n // tn),
        in_specs=[pl.BlockSpec((tm, kdim), lambda i, j: (i, 0)),
                  b_spec,
                  pl.BlockSpec((1, tn), lambda i, j: (0, j))],
        out_specs=pl.BlockSpec((tm, tn), lambda i, j: (i, j)),
        out_shape=jax.ShapeDtypeStruct((m, n), out_dtype),
        compiler_params=_params("parallel", "arbitrary"),
        name=name,
    )(a, b, bias)


def _router_kernel(x_ref, w_ref, b_ref, o_ref):
    x = x_ref[...]
    w = w_ref[...]
    xh = x.astype(BF16)
    xl = (x - xh.astype(F32)).astype(BF16)
    wh = w.astype(BF16)
    wl = (w - wh.astype(F32)).astype(BF16)
    acc = jnp.dot(xh, wh, preferred_element_type=F32)
    acc += jnp.dot(xh, wl, preferred_element_type=F32)
    acc += jnp.dot(xl, wh, preferred_element_type=F32)
    o_ref[...] = acc + b_ref[...]


def router_logits(tokens, router_w, router_b):
    n_tok = tokens.shape[0]
    npad = 128
    w = jnp.zeros((D_MODEL, npad), F32).at[:, :N_EXPERTS].set(router_w)
    b = jnp.zeros((1, npad), F32).at[0, :N_EXPERTS].set(router_b)
    out = pl.pallas_call(
        _router_kernel,
        grid=(n_tok // ROW_TILE,),
        in_specs=[pl.BlockSpec((ROW_TILE, D_MODEL), lambda i: (i, 0)),
                  pl.BlockSpec((D_MODEL, npad), lambda i: (0, 0)),
                  pl.BlockSpec((1, npad), lambda i: (0, 0))],
        out_specs=pl.BlockSpec((ROW_TILE, npad), lambda i: (i, 0)),
        out_shape=jax.ShapeDtypeStruct((n_tok, npad), F32),
        compiler_params=_params("parallel"),
        name="router",
    )(tokens, w, b)
    return out[:, :N_EXPERTS]


def _mod_index(i):
    return (i // LAT_TILES, 0, 0)


def _modulate_kernel(x_ref, sc_ref, sh_ref, o_ref):
    o_ref[...] = (x_ref[...] * (1.0 + sc_ref[0]) + sh_ref[0]).astype(o_ref.dtype)


def modulate(x, sc, sh, out_dtype=BF16):
    n = x.shape[0]
    return pl.pallas_call(
        _modulate_kernel,
        grid=(n // ROW_TILE,),
        in_specs=[pl.BlockSpec((ROW_TILE, D_MODEL), lambda i: (i, 0)),
                  pl.BlockSpec((1, 1, D_MODEL), _mod_index),
                  pl.BlockSpec((1, 1, D_MODEL), _mod_index)],
        out_specs=pl.BlockSpec((ROW_TILE, D_MODEL), lambda i: (i, 0)),
        out_shape=jax.ShapeDtypeStruct((n, D_MODEL), out_dtype),
        compiler_params=_params("parallel"),
        name="modulate",
    )(x, sc, sh)


def _layernorm(z, g, b):
    mu = jnp.mean(z, axis=-1, keepdims=True)
    zc = z - mu
    var = jnp.mean(zc * zc, axis=-1, keepdims=True)
    return zc * lax.rsqrt(var + LN_EPS) * g + b


def _postln_mod_kernel(x_ref, m_ref, g_ref, sc_ref, sh_ref, lng_ref, lnb_ref, y_ref, t_ref, tb_ref):
    y = _layernorm(DEEPNORM_ALPHA * x_ref[...] + g_ref[0] * m_ref[...], lng_ref[...], lnb_ref[...])
    y_ref[...] = y
    t = y * (1.0 + sc_ref[0]) + sh_ref[0]
    t_ref[...] = t
    for s in range(TOKEN_SLAB[0]):
        tb_ref[:, s, :] = t[:, s * TOKEN_SLAB[1]:(s + 1) * TOKEN_SLAB[1]]


def post_ln(x, m, g, ln_g, ln_b, sc, sh):
    n = x.shape[0]
    row = pl.BlockSpec((ROW_TILE, D_MODEL), lambda i: (i, 0))
    mod = pl.BlockSpec((1, 1, D_MODEL), _mod_index)
    vec = pl.BlockSpec((1, D_MODEL), lambda i: (0, 0))
    f32_rows = jax.ShapeDtypeStruct((n, D_MODEL), F32)
    return pl.pallas_call(
        _postln_mod_kernel,
        grid=(n // ROW_TILE,),
        in_specs=[row, row, mod, mod, mod, vec, vec],
        out_specs=[row, row, pl.BlockSpec((ROW_TILE,) + TOKEN_SLAB, lambda i: (i, 0, 0))],
        out_shape=[f32_rows, f32_rows, jax.ShapeDtypeStruct((n,) + TOKEN_SLAB, F32)],
        compiler_params=_params("parallel"),
        name="post_ln_mod",
    )(x, m, g, sc, sh, ln_g.reshape(1, D_MODEL), ln_b.reshape(1, D_MODEL))


def _attn_kernel(q_ref, k_ref, v_ref, o_ref, *, nkv, tk, unroll):
    g, tq, d = q_ref.shape[1:]
    dv = v_ref.shape[-1]
    rows = g * tq
    q = q_ref[0].reshape(rows, d)

    def body(j, carry):
        m_prev, l_prev, acc = carry
        start = pl.multiple_of(j * tk, tk)
        kj = k_ref[0, pl.ds(start, tk), :]
        vj = v_ref[0, pl.ds(start, tk), :]
        s = lax.dot_general(q, kj, (((1,), (1,)), ((), ())), preferred_element_type=F32)
        m_new = jnp.maximum(m_prev, jnp.max(s, axis=-1, keepdims=True))
        alpha = jnp.exp(m_prev - m_new)
        p = jnp.exp(s - m_new)
        l_new = alpha * l_prev + jnp.sum(p, axis=-1, keepdims=True)
        acc = alpha * acc + jnp.dot(p.astype(BF16), vj, preferred_element_type=F32)
        return m_new, l_new, acc

    init = (jnp.full((rows, 1), -jnp.inf, F32), jnp.zeros((rows, 1), F32), jnp.zeros((rows, dv), F32))
    _, l, acc = lax.fori_loop(0, nkv, body, init, unroll=unroll)
    o = acc / l
    for gi in range(g):
        o_ref[:, gi * dv:(gi + 1) * dv] = o[gi * tq:(gi + 1) * tq].astype(o_ref.dtype)


def attention(q, k, v, q_rows, kv_rows, tq, tk, unroll=2, name="attn"):
    hkv, g, _, d = q.shape
    dv = v.shape[-1]
    q0, nq = q_rows
    k0, nk = kv_rows
    assert q0 % tq == 0 and nq % tq == 0 and k0 % nk == 0 and nk % tk == 0
    qb, kb = q0 // tq, k0 // nk
    return pl.pallas_call(
        functools.partial(_attn_kernel, nkv=nk // tk, tk=tk, unroll=unroll),
        grid=(hkv, nq // tq),
        in_specs=[pl.BlockSpec((1, g, tq, d), lambda h, i: (h, 0, i + qb, 0)),
                  pl.BlockSpec((1, nk, d), lambda h, i: (h, kb, 0)),
                  pl.BlockSpec((1, nk, dv), lambda h, i: (h, kb, 0))],
        out_specs=pl.BlockSpec((tq, g * dv), lambda h, i: (i, h)),
        out_shape=jax.ShapeDtypeStruct((nq, hkv * g * dv), BF16),
        compiler_params=_params("parallel", "arbitrary"),
        name=name,
    )(q, k, v)


def lat_ctx_attention(q, k, v, tq, name):
    o_l = attention(q, k, v, (0, SEQ), (0, N_TOK), tq, ATTN_KV_TILE, unroll=N_TOK // ATTN_KV_TILE,
                    name=name + "_lat")
    o_c = attention(q, k, v, (SEQ, CTX_LEN), (SEQ, CTX_LEN), min(tq, CTX_LEN), CTX_LEN, unroll=1,
                    name=name + "_ctx")
    return jnp.concatenate([o_l, o_c], axis=0)


def _bd(x, m_left, m_right):
    return jnp.concatenate([x * m_left, x * m_right], axis=0)


def _cumsum_f32(tri, x):
    x1 = x.astype(BF16)
    r1 = x - x1.astype(F32)
    x2 = r1.astype(BF16)
    x3 = (r1 - x2.astype(F32)).astype(BF16)
    t = tri.astype(BF16)
    return (jnp.dot(t, x1, preferred_element_type=F32) + jnp.dot(t, x2, preferred_element_type=F32)
            + jnp.dot(t, x3, preferred_element_type=F32))


def _wkv_prep_kernel(r_ref, lw_ref, k_ref, v_ref, a_ref, b_ref,
                     wr_ref, u_ref, y0_ref, mrb_ref, bh_ref, g_ref, pc_ref, *, reverse):
    c = WKV_CHUNK
    n = RWKV_HEAD_DIM
    row = lax.broadcasted_iota(jnp.int32, (c, c), 0)
    col = lax.broadcasted_iota(jnp.int32, (c, c), 1)
    tri = ((col >= row) if reverse else (col <= row)).astype(F32)
    row2 = lax.broadcasted_iota(jnp.int32, (2 * c, 2 * c), 0) % c
    col2 = lax.broadcasted_iota(jnp.int32, (2 * c, 2 * c), 1) % c
    if reverse:
        incl, strict = col2 >= row2, col2 > row2
    else:
        incl, strict = col2 <= row2, col2 < row2
    lane = lax.broadcasted_iota(jnp.int32, (1, 2 * n), 1)
    m_left = (lane < n).astype(F32)
    m_right = 1.0 - m_left
    bd = functools.partial(_bd, m_left=m_left, m_right=m_right)
    last = 0 if reverse else c - 1

    chunks = range(WKV_PREP_CHUNKS)
    rows = [slice(ci * c, (ci + 1) * c) for ci in chunks]
    lw = [lw_ref[rw, :] for rw in rows]
    cum = [_cumsum_f32(tri, z) for z in lw]
    p_incl = [jnp.exp(z) for z in cum]
    p_inv = [jnp.exp(-z) for z in cum]
    pe = [z[last:last + 1, :] for z in p_incl]
    at = [bd(a_ref[rows[ci], :] * jnp.exp(cum[ci] - lw[ci])) for ci in chunks]
    rt = [bd(r_ref[rows[ci], :] * p_incl[ci]) for ci in chunks]
    bt = [bd(b_ref[rows[ci], :] * p_inv[ci]) for ci in chunks]
    kt = [bd(k_ref[rows[ci], :] * p_inv[ci]) for ci in chunks]
    v = [bd(v_ref[rows[ci], :]) for ci in chunks]
    big = [_dot_t(jnp.concatenate([at[ci], rt[ci]], axis=0), jnp.concatenate([bt[ci], kt[ci]], axis=0))
           for ci in chunks]
    l_pow = [jnp.where(strict, z[:2 * c, :2 * c], 0.0) for z in big]
    lm = [jnp.concatenate([jnp.where(strict, z[:2 * c, 2 * c:], 0.0),
                           jnp.where(incl, z[2 * c:, 2 * c:], 0.0)], axis=0) for z in big]
    for ci in chunks:
        wr_ref[ci, 0, 2 * c:, :] = rt[ci].astype(BF16)
        mrb_ref[ci, 0] = jnp.where(incl, big[ci][2 * c:, :2 * c], 0.0).astype(BF16)
        bh_ref[ci, 0] = (bt[ci] * pe[ci]).astype(BF16)
        pc_ref[ci, 0] = jnp.broadcast_to(pe[ci], (8, 2 * n))
    lv = [_dot(lm[ci], v[ci]) for ci in chunks]
    for ci in chunks:
        y0_ref[ci, 0] = lv[ci][2 * c:]
        g_ref[ci, 0] = _tdot(v[ci], kt[ci] * pe[ci])
    x = [jnp.concatenate([at[ci], lv[ci][:2 * c]], axis=1) for ci in chunks]
    x = [x[ci] + _dot(l_pow[ci], x[ci]) for ci in chunks]
    span = 1
    while span * 2 < c:
        l_pow = [_dot(z, z) for z in l_pow]
        x = [x[ci] + _dot(l_pow[ci], x[ci]) for ci in chunks]
        span *= 2
    for ci in chunks:
        wr_ref[ci, 0, :2 * c, :] = x[ci][:, :2 * n].astype(BF16)
        u_ref[ci, 0] = x[ci][:, 2 * n:]


def _wkv_chunk_fwd(s):
    return (s + SEQ // WKV_CHUNK) % WKV_CHUNKS


def _wkv_chunk_bwd(s):
    return WKV_CHUNKS - 1 - s


def wkv_prep(r, lw, k, v, a, b, reverse):
    c, n2 = WKV_CHUNK, 2 * RWKV_HEAD_DIM
    cb = WKV_PREP_CHUNKS
    spec = pl.BlockSpec((cb * c, n2), lambda p, i: (i, p))

    def out(rows, dtype):
        return (pl.BlockSpec((cb, 1, rows, n2), lambda p, i: (i, p, 0, 0)),
                jax.ShapeDtypeStruct((WKV_CHUNKS, WKV_PAIRS, rows, n2), dtype))

    outs = [out(4 * c, BF16), out(2 * c, F32), out(2 * c, F32), out(2 * c, BF16), out(2 * c, BF16),
            out(2 * c, F32), out(8, F32)]
    return pl.pallas_call(
        functools.partial(_wkv_prep_kernel, reverse=reverse),
        grid=(WKV_PAIRS, WKV_CHUNKS // cb),
        in_specs=[spec] * 6,
        out_specs=[o[0] for o in outs],
        out_shape=[o[1] for o in outs],
        compiler_params=_params("parallel", "parallel"),
        name="wkv_prep_bwd" if reverse else "wkv_prep_fwd",
    )(r, lw, k, v, a, b)


def _wkv_scan_kernel(*refs):
    c, n = WKV_CHUNK, RWKV_HEAD_DIM
    ins, (yf_ref, yb_ref, s_ref) = refs[:14], refs[14:]

    @pl.when(pl.program_id(0) == 0)
    def _():
        s_ref[...] = jnp.zeros_like(s_ref)

    jobs = [(d, p) for d in range(2) for p in range(WKV_PAIRS)]
    ref = lambda d, k: ins[7 * d + k]
    s0 = [s_ref[d, p] for d, p in jobs]
    z = [_dot_t(ref(d, 0)[0, p], s0[j]) for j, (d, p) in enumerate(jobs)]
    sa = [z[j][:2 * c] + ref(d, 1)[0, p] for j, (d, p) in enumerate(jobs)]
    for j, (d, p) in enumerate(jobs):
        s_ref[d, p] = s0[j] * ref(d, 6)[0, p, 0:1, :] + ref(d, 5)[0, p] + _tdot(sa[j], ref(d, 4)[0, p])
    for j, (d, p) in enumerate(jobs):
        y = z[j][2 * c:] + ref(d, 2)[0, p] + _dot(ref(d, 3)[0, p], sa[j])
        (yf_ref, yb_ref)[d][:, p * 2 * n:(p + 1) * 2 * n] = y[:c] + y[c:]


def wkv_scan(prep_f, prep_b):
    c, n2 = WKV_CHUNK, 2 * RWKV_HEAD_DIM

    def specs(chunk_of):
        return [pl.BlockSpec((1, WKV_PAIRS) + a.shape[2:], lambda s: (chunk_of(s), 0, 0, 0)) for a in prep_f]

    y_shape = jax.ShapeDtypeStruct((N_TOK, RWKV_WIDTH), F32)
    return pl.pallas_call(
        _wkv_scan_kernel,
        grid=(WKV_CHUNKS,),
        in_specs=specs(_wkv_chunk_fwd) + specs(_wkv_chunk_bwd),
        out_specs=[pl.BlockSpec((c, RWKV_WIDTH), lambda s: (_wkv_chunk_fwd(s), 0)),
                   pl.BlockSpec((c, RWKV_WIDTH), lambda s: (_wkv_chunk_bwd(s), 0))],
        out_shape=[y_shape, y_shape],
        scratch_shapes=[pltpu.VMEM((2, WKV_PAIRS, n2, n2), F32)],
        compiler_params=_params("arbitrary"),
        name="wkv_scan",
    )(*prep_f, *prep_b)


MOE_CAST_ROWS = 256


def _moe_kernel(be_ref, idx_ref, idx_next_ref, tok_hbm, w1_ref, b1_ref, w2_ref, b2_ref, g_ref, o_ref,
                w1b_ref, w2b_ref, xbuf, sem):
    i = pl.program_id(0)
    cur = i % 2

    def row_copy(idx, r, buf):
        return pltpu.make_async_copy(tok_hbm.at[idx[0, 0, r]], xbuf.at[buf, r], sem.at[buf])

    def start_rows(idx, buf):
        def body(r, carry):
            row_copy(idx, r, buf).start()
            return carry
        lax.fori_loop(0, DISPATCH_BLOCK, body, 0, unroll=8)

    @pl.when(i == 0)
    def _():
        start_rows(idx_ref, 0)

    @pl.when(i + 1 < pl.num_programs(0))
    def _():
        start_rows(idx_next_ref, 1 - cur)

    new_expert = (i == 0) | (be_ref[i] != be_ref[jnp.maximum(i - 1, 0)])

    @pl.when(new_expert)
    def _():
        def cast1(c, carry):
            rows = pl.ds(pl.multiple_of(c * MOE_CAST_ROWS, MOE_CAST_ROWS), MOE_CAST_ROWS)
            w1b_ref[rows, :] = w1_ref[0, rows, :].astype(BF16)
            return carry

        def cast2(c, carry):
            rows = pl.ds(pl.multiple_of(c * MOE_CAST_ROWS, MOE_CAST_ROWS), MOE_CAST_ROWS)
            w2b_ref[rows, :] = w2_ref[0, rows, :].astype(BF16)
            return carry

        lax.fori_loop(0, D_MODEL // MOE_CAST_ROWS, cast1, 0)
        lax.fori_loop(0, EXPERT_FF // MOE_CAST_ROWS, cast2, 0)

    def wait_row(r, carry):
        row_copy(idx_ref, r, cur).wait()
        return carry
    lax.fori_loop(0, DISPATCH_BLOCK, wait_row, 0, unroll=8)

    x = xbuf[cur].reshape(DISPATCH_BLOCK, D_MODEL).astype(BF16)
    u = jnp.dot(x, w1b_ref[...], preferred_element_type=F32) + b1_ref[0]
    u_glu = jnp.minimum(u[:, :EXPERT_FF], SWIGLU_LIMIT)
    u_lin = jnp.clip(u[:, EXPERT_FF:], -SWIGLU_LIMIT, SWIGLU_LIMIT)
    act = (u_glu * jax.nn.sigmoid(SWIGLU_ALPHA * u_glu) * (u_lin + 1.0)).astype(BF16)
    gate = g_ref[...]
    ew = EXPERT_SLAB[1]
    for s in range(EXPERT_SLAB[0]):
        cols = slice(s * ew, (s + 1) * ew)
        y = jnp.dot(act, w2b_ref[:, cols], preferred_element_type=F32) + b2_ref[0, :, cols]
        o_ref[:, s, :] = y * gate


def moe_experts(token_slabs, slot_token, slot_gate, block_expert, w1, b1, w2, b2, layer):
    n_slots = slot_token.shape[0]
    n_blocks = n_slots // DISPATCH_BLOCK
    idx = slot_token.reshape(n_blocks, 1, DISPATCH_BLOCK)
    idx_spec = lambda nxt: pl.BlockSpec((1, 1, DISPATCH_BLOCK),
                                        lambda i, be: (jnp.minimum(i + nxt, n_blocks - 1), 0, 0),
                                        memory_space=pltpu.SMEM)
    grid_spec = pltpu.PrefetchScalarGridSpec(
        num_scalar_prefetch=1,
        grid=(n_blocks,),
        in_specs=[idx_spec(0), idx_spec(1),
                  pl.BlockSpec(memory_space=pl.ANY),
                  pl.BlockSpec((None, 1, D_MODEL, 2 * EXPERT_FF), lambda i, be: (layer, be[i], 0, 0),
                               pipeline_mode=pl.Buffered(1)),
                  pl.BlockSpec((1, 1, 2 * EXPERT_FF), lambda i, be: (be[i], 0, 0)),
                  pl.BlockSpec((None, 1, EXPERT_FF, D_MODEL), lambda i, be: (layer, be[i], 0, 0),
                               pipeline_mode=pl.Buffered(1)),
                  pl.BlockSpec((1, 1, D_MODEL), lambda i, be: (be[i], 0, 0)),
                  pl.BlockSpec((DISPATCH_BLOCK, 1), lambda i, be: (i, 0))],
        out_specs=pl.BlockSpec((DISPATCH_BLOCK,) + EXPERT_SLAB, lambda i, be: (i, 0, 0)),
        scratch_shapes=[pltpu.VMEM((D_MODEL, 2 * EXPERT_FF), BF16), pltpu.VMEM((EXPERT_FF, D_MODEL), BF16),
                        pltpu.VMEM((2, DISPATCH_BLOCK) + TOKEN_SLAB, F32), pltpu.SemaphoreType.DMA((2,))],
    )
    return pl.pallas_call(
        _moe_kernel,
        grid_spec=grid_spec,
        out_shape=jax.ShapeDtypeStruct((n_slots,) + EXPERT_SLAB, F32),
        compiler_params=_params("arbitrary", vmem=MOE_VMEM_LIMIT),
        name="moe_experts",
    )(block_expert, idx, idx, token_slabs, w1, b1.reshape(N_EXPERTS, 1, -1), w2, b2.reshape(N_EXPERTS, 1, -1),
      slot_gate.reshape(n_slots, 1))


COMBINE_ROWS = 64


def _combine_ln_kernel(idx_ref, idx_next_ref, y_hbm, x_ref, g_ref, lng_ref, lnb_ref, o_ref, buf, sem):
    i = pl.program_id(0)
    cur = i % 2

    def row_copy(idx, r, j, b):
        return pltpu.make_async_copy(y_hbm.at[idx[0, 0, r * TOP_K + j]], buf.at[b, j, r], sem.at[b])

    def start_rows(idx, b):
        def body(r, carry):
            for j in range(TOP_K):
                row_copy(idx, r, j, b).start()
            return carry
        lax.fori_loop(0, COMBINE_ROWS, body, 0, unroll=4)

    @pl.when(i == 0)
    def _():
        start_rows(idx_ref, 0)

    @pl.when(i + 1 < pl.num_programs(0))
    def _():
        start_rows(idx_next_ref, 1 - cur)

    def wait_rows(r, carry):
        for j in range(TOP_K):
            row_copy(idx_ref, r, j, cur).wait()
        return carry
    lax.fori_loop(0, COMBINE_ROWS, wait_rows, 0, unroll=4)

    f = buf[cur, 0]
    for j in range(1, TOP_K):
        f = f + buf[cur, j]
    f = f.reshape(COMBINE_ROWS, D_MODEL)
    o_ref[...] = _layernorm(DEEPNORM_ALPHA * x_ref[...] + g_ref[0] * f, lng_ref[...], lnb_ref[...])


def moe_combine_ln(y, slot, x, g, ln_g, ln_b):
    n = x.shape[0]
    nb = n // COMBINE_ROWS
    idx = slot.reshape(nb, 1, COMBINE_ROWS * TOP_K)
    idx_spec = lambda nxt: pl.BlockSpec((1, 1, COMBINE_ROWS * TOP_K),
                                        lambda i: (jnp.minimum(i + nxt, nb - 1), 0, 0),
                                        memory_space=pltpu.SMEM)
    row = pl.BlockSpec((COMBINE_ROWS, D_MODEL), lambda i: (i, 0))
    mod = pl.BlockSpec((1, 1, D_MODEL), lambda i: (i // (SEQ // COMBINE_ROWS), 0, 0))
    vec = pl.BlockSpec((1, D_MODEL), lambda i: (0, 0))
    return pl.pallas_call(
        _combine_ln_kernel,
        grid=(nb,),
        in_specs=[idx_spec(0), idx_spec(1), pl.BlockSpec(memory_space=pl.ANY), row, mod, vec, vec],
        out_specs=row,
        out_shape=jax.ShapeDtypeStruct((n, D_MODEL), F32),
        scratch_shapes=[pltpu.VMEM((2, TOP_K, COMBINE_ROWS) + EXPERT_SLAB, F32), pltpu.SemaphoreType.DMA((2,))],
        compiler_params=_params("arbitrary"),
        name="moe_combine_ln",
    )(idx, idx, y, x, g, ln_g.reshape(1, D_MODEL), ln_b.reshape(1, D_MODEL))


def moe(tokens, token_slabs, router_w, router_b, w1, b1, w2, b2, layer):
    n_tok = tokens.shape[0]
    n_assign = n_tok * TOP_K
    n_blocks = -(-(n_assign + N_EXPERTS * (DISPATCH_BLOCK - 1)) // DISPATCH_BLOCK)
    n_slots = n_blocks * DISPATCH_BLOCK
    logits = router_logits(tokens, router_w, router_b)
    top_val, top_idx = lax.top_k(logits, TOP_K)
    gate = jax.nn.softmax(top_val, axis=-1).reshape(-1)
    expert = top_idx.reshape(-1)
    onehot = (expert[:, None] == jnp.arange(N_EXPERTS, dtype=expert.dtype)[None, :]).astype(jnp.int32)
    before = jnp.cumsum(onehot, axis=0) - onehot
    counts = before[-1] + onehot[-1]
    padded = (counts + DISPATCH_BLOCK - 1) // DISPATCH_BLOCK * DISPATCH_BLOCK
    pad_end = jnp.cumsum(padded)
    pad_start = pad_end - padded
    slot = jnp.sum(onehot * (before + pad_start[None, :]), axis=1).astype(jnp.int32)
    slot_assign = jnp.full((n_slots,), -1, jnp.int32).at[slot].set(jnp.arange(n_assign, dtype=jnp.int32))
    filled = slot_assign >= 0
    src = jnp.maximum(slot_assign, 0)
    slot_token = jnp.where(filled, src // TOP_K, 0)
    slot_gate = jnp.where(filled, gate[src], 0.0)
    block_start = jnp.arange(n_blocks, dtype=pad_end.dtype) * DISPATCH_BLOCK
    block_expert = jnp.minimum(jnp.sum(pad_end[None, :] <= block_start[:, None], axis=1),
                               N_EXPERTS - 1).astype(jnp.int32)
    y = moe_experts(token_slabs, slot_token, slot_gate, block_expert, w1, b1, w2, b2, layer)
    return y, slot.reshape(n_tok, TOP_K)


def _rope_lanes(x, cos, sin, half):
    lane = lax.broadcasted_iota(jnp.int32, x.shape, 1)
    up = pltpu.roll(x, 128 - half, 1)
    down = pltpu.roll(x, half, 1)
    return x * cos + jnp.where(lane % (2 * half) < half, up, down) * sin


def _rms(x, g):
    return x * lax.rsqrt(jnp.mean(x * x, axis=-1, keepdims=True) + RMS_EPS) * g


def _attn_prep_kernel(pa_ref, pb_ref, pc_ref, gq_ref, gk_ref, gmq_ref, gmkv_ref, cosa_ref, sina_ref,
                      cosb_ref, sinb_ref, wuq_ref, wukv_ref, qg_ref, kg_ref, vg_ref, qc_ref, kc_ref, vm_ref):
    hd = GQA_HEAD_DIM
    ca, sa, cb, sb = cosa_ref[...], sina_ref[...], cosb_ref[...], sinb_ref[...]
    grp = GQA_HEADS // GQA_KV_HEADS
    for h in range(GQA_HEADS):
        q = _rope_lanes(_rms(pa_ref[:, h * hd:(h + 1) * hd], gq_ref[...]), ca, sa, hd // 4)
        qg_ref[h // grp, h % grp] = (q * GQA_HEAD_DIM ** -0.5).astype(BF16)
    for h in range(GQA_KV_HEADS):
        k = _rope_lanes(_rms(pb_ref[:, h * hd:(h + 1) * hd], gk_ref[...]), ca, sa, hd // 4)
        kg_ref[h] = k.astype(BF16)
        vg_ref[h] = pb_ref[:, (GQA_KV_HEADS + h) * hd:(GQA_KV_HEADS + h + 1) * hd].astype(BF16)
    dkvn = _rms(pb_ref[:, 2 * GQA_KV_HEADS * hd:], gmkv_ref[...])
    dqn = _rms(pc_ref[:, :MLA_Q_RANK], gmq_ref[...])
    k_rope = _rope_lanes(pc_ref[:, MLA_Q_RANK:], cb, sb, MLA_ROPE_DIM // 4)
    qb = _dot(dqn, wuq_ref[...])
    kvb = _dot(dkvn, wukv_ref[...])
    scale = MLA_QK_DIM ** -0.5
    for h in range(MLA_HEADS):
        base = h * MLA_CAT_DIM
        q_nope = qb[:, base:base + 128]
        q_rope = _rope_lanes(qb[:, base + 128:base + 256], cb, sb, MLA_ROPE_DIM // 4)
        qc_ref[h, 0] = (jnp.concatenate([q_nope, q_rope], axis=1) * scale).astype(BF16)
        kc_ref[h] = jnp.concatenate([kvb[:, base:base + 128], k_rope], axis=1).astype(BF16)
        vm_ref[h] = kvb[:, base + 128:base + 256].astype(BF16)


def _rope_lane_tables(rot_dim):
    n_freq = rot_dim // 4
    row = jnp.repeat(jnp.arange(SEQ // GRID_W, dtype=F32), GRID_W)
    col = (jnp.arange(SEQ) % GRID_W).astype(F32)
    freqs = ROPE_THETA ** (-jnp.arange(n_freq, dtype=F32) / n_freq)
    ar, ac = row[:, None] * freqs, col[:, None] * freqs
    cos = jnp.concatenate([jnp.cos(ar), jnp.cos(ar), jnp.cos(ac), jnp.cos(ac)], axis=1)
    sin = jnp.concatenate([-jnp.sin(ar), jnp.sin(ar), -jnp.sin(ac), jnp.sin(ac)], axis=1)
    cos = jnp.pad(cos, ((0, CTX_LEN), (0, 128 - rot_dim)), constant_values=1.0)
    sin = jnp.pad(sin, ((0, CTX_LEN), (0, 128 - rot_dim)))
    return cos, sin


def attn_prep(proj, gqa_q_norm, gqa_k_norm, mla_q_norm, mla_kv_norm, mla_w_uq, mla_w_ukv):
    t = N_TOK
    cos_a, sin_a = _rope_lane_tables(GQA_HEAD_DIM)
    cos_b, sin_b = _rope_lane_tables(MLA_ROPE_DIM)
    w_uq = jnp.pad(mla_w_uq.reshape(MLA_Q_RANK, MLA_HEADS, MLA_QK_DIM),
                   ((0, 0), (0, 0), (0, MLA_CAT_DIM - MLA_QK_DIM))).reshape(MLA_Q_RANK, -1).astype(BF16)
    w_ukv = mla_w_ukv.astype(BF16)
    first = SMALL_COL0 // 1024
    col = lambda j: pl.BlockSpec((ROW_TILE, 1024), lambda i: (i, first + j))
    vec = lambda n: pl.BlockSpec((1, n), lambda i: (0, 0))
    tab = pl.BlockSpec((ROW_TILE, 128), lambda i: (i, 0))
    full = lambda a: pl.BlockSpec(a.shape, lambda i: (0, 0))
    grp = GQA_HEADS // GQA_KV_HEADS
    hd = GQA_HEAD_DIM
    outs = [
        (pl.BlockSpec((GQA_KV_HEADS, grp, ROW_TILE, hd), lambda i: (0, 0, i, 0)), (GQA_KV_HEADS, grp, t, hd)),
        (pl.BlockSpec((GQA_KV_HEADS, ROW_TILE, hd), lambda i: (0, i, 0)), (GQA_KV_HEADS, t, hd)),
        (pl.BlockSpec((GQA_KV_HEADS, ROW_TILE, hd), lambda i: (0, i, 0)), (GQA_KV_HEADS, t, hd)),
        (pl.BlockSpec((MLA_HEADS, 1, ROW_TILE, MLA_CAT_DIM), lambda i: (0, 0, i, 0)), (MLA_HEADS, 1, t, MLA_CAT_DIM)),
        (pl.BlockSpec((MLA_HEADS, ROW_TILE, MLA_CAT_DIM), lambda i: (0, i, 0)), (MLA_HEADS, t, MLA_CAT_DIM)),
        (pl.BlockSpec((MLA_HEADS, ROW_TILE, MLA_V_DIM), lambda i: (0, i, 0)), (MLA_HEADS, t, MLA_V_DIM)),
    ]
    return pl.pallas_call(
        _attn_prep_kernel,
        grid=(t // ROW_TILE,),
        in_specs=[col(0), col(1), col(2), vec(hd), vec(hd), vec(MLA_Q_RANK), vec(MLA_KV_RANK),
                  tab, tab, tab, tab, full(w_uq), full(w_ukv)],
        out_specs=[o[0] for o in outs],
        out_shape=[jax.ShapeDtypeStruct(o[1], BF16) for o in outs],
        compiler_params=_params("parallel"),
        name="attn_prep",
    )(proj, proj, proj, gqa_q_norm.reshape(1, -1), gqa_k_norm.reshape(1, -1), mla_q_norm.reshape(1, -1),
      mla_kv_norm.reshape(1, -1), cos_a, sin_a, cos_b, sin_b, w_uq, w_ukv)


def _group_sum(x, ones_bd):
    hi = x.astype(BF16)
    lo = (x - hi.astype(F32)).astype(BF16)
    return jnp.dot(hi, ones_bd, preferred_element_type=F32) + jnp.dot(lo, ones_bd, preferred_element_type=F32)


def _softplus(z):
    return jnp.maximum(z, 0.0) + jnp.log(1.0 + jnp.exp(-jnp.abs(z)))


def _rwkv_prep_kernel(p_ref, prev_ref, next_ref, mu_ref, wup_ref, w0_ref, aup_ref, a0_ref, gup_ref,
                      kk_ref, ka_ref, rk_ref, ones_ref,
                      r_ref, v_ref, a_ref, lwf_ref, kdf_ref, bf_ref, lwb_ref, kdb_ref, bb_ref, g_ref, bonus_ref):
    i = pl.program_id(0)
    w = RWKV_WIDTH
    p = p_ref[...]
    n = p.shape[0]
    rowid = lax.broadcasted_iota(jnp.int32, (n, 1), 0)
    has_prev = jnp.where(i % LAT_TILES == 0, 0.0, 1.0)
    has_next = jnp.where((i == LAT_TILES - 1) | (i == LAT_TILES), 0.0, 1.0)
    prev = jnp.where(rowid == 0, prev_ref[7:8, :] * has_prev, pltpu.roll(p, 1, 0))
    nxt = jnp.where(rowid == n - 1, next_ref[0:1, :] * has_next, pltpu.roll(p, n - 1, 0))
    p = p + mu_ref[0:1, :] * (prev - p) + mu_ref[1:2, :] * (nxt - p)

    ones_bd = ones_ref[...]
    r, kx, vx = p[:, :w], p[:, w:2 * w], p[:, 2 * w:3 * w]
    low = p[:, 3 * w:]
    g_ref[...] = _dot(jax.nn.sigmoid(low[:, 512:]), gup_ref[...])
    kk = kx * kk_ref[...]
    kk = kk / jnp.maximum(jnp.sqrt(_group_sum(kk * kk, ones_bd)), 1e-12)
    r_ref[...] = r
    v_ref[...] = vx
    a_ref[...] = -kk
    k_sum = jnp.zeros_like(kx)
    for d, (lw_ref, kd_ref, b_ref) in enumerate(((lwf_ref, kdf_ref, bf_ref), (lwb_ref, kdb_ref, bb_ref))):
        wd = low[:, d * 128:(d + 1) * 128]
        ad = low[:, 256 + d * 128:256 + (d + 1) * 128]
        w_lin = _dot(jnp.tanh(wd), wup_ref[d]) + w0_ref[d:d + 1, :]
        lw_ref[...] = -jnp.exp(-_softplus(-w_lin) - 0.5)
        a_gate = jax.nn.sigmoid(_dot(ad, aup_ref[d]) + a0_ref[d:d + 1, :])
        k_d = kx * (1.0 + (a_gate - 1.0) * ka_ref[...])
        kd_ref[...] = k_d
        b_ref[...] = kk * a_gate
        k_sum = k_sum + k_d
    bonus_ref[...] = _group_sum(r * k_sum * rk_ref[...], ones_bd) * vx


def _head_ones():
    h = jnp.arange(RWKV_WIDTH) // RWKV_HEAD_DIM
    return (h[:, None] == h[None, :]).astype(BF16)


def rwkv_prep(proj, rwkv_mu, rwkv_w0, rwkv_w_up, rwkv_a0, rwkv_a_up, rwkv_g_up, rwkv_k_k, rwkv_k_a, rwkv_r_k):
    t, w = N_TOK, RWKV_WIDTH
    cb = RWKV_COL0 // RWKV_IN_PAD
    n8 = t // 8
    per8 = ROW_TILE // 8
    mu = jnp.pad(rwkv_mu, ((0, 6), (0, RWKV_IN_PAD - RWKV_IN)))
    g_up = jnp.pad(rwkv_g_up, ((0, 512 - RWKV_GATE_RANK), (0, 0))).astype(BF16)
    full = lambda a: pl.BlockSpec(a.shape, lambda i: (0,) * a.ndim)
    consts = [mu, rwkv_w_up.astype(BF16), rwkv_w0, rwkv_a_up.astype(BF16), rwkv_a0, g_up,
              rwkv_k_k.reshape(1, w), rwkv_k_a.reshape(1, w), rwkv_r_k.reshape(1, w), _head_ones()]
    out_spec = pl.BlockSpec((ROW_TILE, w), lambda i: (i, 0))
    return pl.pallas_call(
        _rwkv_prep_kernel,
        grid=(t // ROW_TILE,),
        in_specs=[pl.BlockSpec((ROW_TILE, RWKV_IN_PAD), lambda i: (i, cb)),
                  pl.BlockSpec((8, RWKV_IN_PAD), lambda i: (jnp.maximum(i * per8 - 1, 0), cb)),
                  pl.BlockSpec((8, RWKV_IN_PAD), lambda i: (jnp.minimum((i + 1) * per8, n8 - 1), cb))]
                 + [full(a) for a in consts],
        out_specs=[out_spec] * 11,
        out_shape=[jax.ShapeDtypeStruct((t, w), F32)] * 11,
        compiler_params=_params("parallel"),
        name="rwkv_prep",
    )(proj, proj, proj, *consts)


def _rwkv_out_kernel(yf_ref, yb_ref, bonus_ref, g_ref, gng_ref, gnb_ref, ones_ref, o_ref):
    ones_bd = ones_ref[...]
    y = yf_ref[...] + yb_ref[...]
    yc = y - _group_sum(y, ones_bd) * (1.0 / RWKV_HEAD_DIM)
    var = _group_sum(yc * yc, ones_bd) * (1.0 / RWKV_HEAD_DIM)
    yn = yc * lax.rsqrt(var + GN_EPS) * gng_ref[...] + gnb_ref[...]
    o_ref[...] = ((yn + bonus_ref[...]) * g_ref[...]).astype(o_ref.dtype)


def rwkv_out(y_f, y_b, bonus, g, gn_g, gn_b):
    t, w = N_TOK, RWKV_WIDTH
    row = pl.BlockSpec((ROW_TILE, w), lambda i: (i, 0))
    vec = pl.BlockSpec((1, w), lambda i: (0, 0))
    ones = _head_ones()
    return pl.pallas_call(
        _rwkv_out_kernel,
        grid=(t // ROW_TILE,),
        in_specs=[row, row, row, row, vec, vec, pl.BlockSpec(ones.shape, lambda i: (0, 0))],
        out_specs=row,
        out_shape=jax.ShapeDtypeStruct((t, w), BF16),
        compiler_params=_params("parallel"),
        name="rwkv_out",
    )(y_f, y_b, bonus, g, gn_g.reshape(1, w), gn_b.reshape(1, w), ones)


def _merge_kernel(oa_ref, ob_ref, oc_ref, wa_ref, wb_ref, wc_ref, ga_ref, gb_ref, gc_ref, m_ref):
    acc = jax.nn.sigmoid(ga_ref[...]) * _dot(oa_ref[...], wa_ref[0])
    acc += jax.nn.sigmoid(gb_ref[...]) * _dot(ob_ref[...], wb_ref[0])
    acc += jax.nn.sigmoid(gc_ref[...]) * _dot(oc_ref[...], wc_ref[0])
    m_ref[...] = acc.astype(m_ref.dtype)


def merge(oa, ob, oc, proj, w_branch_up):
    t = N_TOK
    tm, tn = 768, 512
    nj = D_MODEL // tn
    wb = w_branch_up.astype(BF16)
    o_spec = pl.BlockSpec((tm, BRANCH_WIDTH), lambda i, j: (i, 0))
    w_spec = lambda b: pl.BlockSpec((1, BRANCH_WIDTH, tn), lambda i, j: (b, 0, j))
    g_spec = lambda b: pl.BlockSpec((tm, tn), lambda i, j: (i, b * nj + j))
    return pl.pallas_call(
        _merge_kernel,
        grid=(t // tm, nj),
        in_specs=[o_spec, o_spec, o_spec, w_spec(0), w_spec(1), w_spec(2), g_spec(0), g_spec(1), g_spec(2)],
        out_specs=pl.BlockSpec((tm, tn), lambda i, j: (i, j)),
        out_shape=jax.ShapeDtypeStruct((t, D_MODEL), BF16),
        compiler_params=_params("parallel", "arbitrary"),
        name="merge",
    )(oa, ob, oc, wb, wb, wb, proj, proj, proj)


def _split(x, sizes):
    return jnp.split(x, np.cumsum(sizes)[:-1].tolist(), axis=-1)


def _in_proj_weight(w_in):
    aq, ak, av, dq, dkv, kr, rwkv, gates = _split(w_in, IN_SPLITS)
    small = jnp.concatenate([aq, ak, av, dkv, dq, kr], axis=1)
    pad = lambda z, n: jnp.pad(z, ((0, 0), (0, n - z.shape[1])))
    return jnp.concatenate([gates, pad(rwkv, RWKV_IN_PAD), pad(small, SMALL_WIDTH)], axis=1).astype(BF16)


def _mixer(h, w_in, gqa_q_norm, gqa_k_norm, mla_q_norm, mla_kv_norm, mla_w_uq, mla_w_ukv,
           rwkv_mu, rwkv_w0, rwkv_w_up, rwkv_a0, rwkv_a_up, rwkv_g_up, rwkv_k_k, rwkv_k_a, rwkv_r_k,
           rwkv_gn_g, rwkv_gn_b, w_branch_up, w_o):
    proj = matmul(h, _in_proj_weight(w_in), name="in_proj")

    qg, kg, vg, qc, kc, vm = attn_prep(proj, gqa_q_norm, gqa_k_norm, mla_q_norm, mla_kv_norm, mla_w_uq, mla_w_ukv)
    oa = lat_ctx_attention(qg, kg, vg, 256, "gqa")
    ob = lat_ctx_attention(qc, kc, vm, 1024, "mla")

    r, v, a, lw_f, kd_f, b_f, lw_b, kd_b, b_b, g, bonus = rwkv_prep(
        proj, rwkv_mu, rwkv_w0, rwkv_w_up, rwkv_a0, rwkv_a_up, rwkv_g_up, rwkv_k_k, rwkv_k_a, rwkv_r_k)
    y_f, y_b = wkv_scan(wkv_prep(r, lw_f, kd_f, v, a, b_f, reverse=False),
                        wkv_prep(r, lw_b, kd_b, v, a, b_b, reverse=True))
    oc = rwkv_out(y_f, y_b, bonus, g, rwkv_gn_g, rwkv_gn_b)

    m = merge(oa, ob, oc, proj, w_branch_up)
    return matmul(m, w_o.astype(BF16), name="out_proj")


def kernel(x, c, ctx, c_ctx, w_ada, b_ada, w_in, gqa_q_norm, gqa_k_norm, mla_q_norm, mla_kv_norm, mla_w_uq,
           mla_w_ukv, rwkv_mu, rwkv_w0, rwkv_w_up, rwkv_a0, rwkv_a_up, rwkv_g_up, rwkv_k_k, rwkv_k_a, rwkv_r_k,
           rwkv_gn_g, rwkv_gn_b, w_branch_up, w_o, ln1_g, ln1_b, router_w, router_b, exp_w_in, exp_b_in,
           exp_w_out, exp_b_out, ln2_g, ln2_b):
    xs = jnp.concatenate([x[0], ctx[0]], axis=0)
    cond = jnp.zeros((16, D_MODEL), F32).at[0].set(jax.nn.silu(c[0])).at[1].set(jax.nn.silu(c_ctx))
    for l in range(DEPTH):
        ada = matmul(cond.astype(BF16), w_ada, bias=b_ada[l], tm=16, tn=512, layer=l, name="adaln")
        sh1, sc1, g1, sh2, sc2, g2 = (z[:2].reshape(2, 1, D_MODEL) for z in _split(ada, (D_MODEL,) * 6))
        h = modulate(xs, sc1, sh1)
        m = _mixer(h, w_in[l], gqa_q_norm[l], gqa_k_norm[l], mla_q_norm[l], mla_kv_norm[l], mla_w_uq[l],
                   mla_w_ukv[l], rwkv_mu[l], rwkv_w0[l], rwkv_w_up[l], rwkv_a0[l], rwkv_a_up[l], rwkv_g_up[l],
                   rwkv_k_k[l], rwkv_k_a[l], rwkv_r_k[l], rwkv_gn_g[l], rwkv_gn_b[l], w_branch_up[l], w_o[l])
        xs, tokens, token_slabs = post_ln(xs, m, g1, ln1_g[l], ln1_b[l], sc2, sh2)
        y, slot = moe(tokens, token_slabs, router_w[l], router_b[l], exp_w_in, exp_b_in[l], exp_w_out,
                      exp_b_out[l], l)
        xs = moe_combine_ln(y, slot, xs, g2, ln2_g[l], ln2_b[l])
    return xs[:SEQ].reshape(1, SEQ, D_MODEL)
```

```python
import functools

import jax
import jax.numpy as jnp
import numpy as np
from jax import lax
from jax.experimental import pallas as pl
from jax.experimental.pallas import tpu as pltpu

F32, BF16 = jnp.float32, jnp.bfloat16

D_MODEL = 4096
SEQ = 8192
CTX_LEN = 256
N_TOK = SEQ + CTX_LEN
DEPTH = 2
GRID_W = 64
ROPE_THETA = 10000.0

GQA_HEADS, GQA_KV_HEADS, GQA_HEAD_DIM = 8, 2, 128
MLA_HEADS, MLA_Q_RANK, MLA_KV_RANK = 8, 896, 512
MLA_NOPE_DIM, MLA_ROPE_DIM, MLA_V_DIM = 128, 64, 128
MLA_QK_DIM = MLA_NOPE_DIM + MLA_ROPE_DIM
MLA_CAT_DIM = 256
RWKV_HEADS, RWKV_HEAD_DIM = 16, 64
RWKV_WIDTH = RWKV_HEADS * RWKV_HEAD_DIM
RWKV_DECAY_RANK, RWKV_ICLR_RANK, RWKV_GATE_RANK = 128, 128, 480
RWKV_SPLITS = (RWKV_WIDTH, RWKV_WIDTH, RWKV_WIDTH, RWKV_DECAY_RANK, RWKV_DECAY_RANK,
               RWKV_ICLR_RANK, RWKV_ICLR_RANK, RWKV_GATE_RANK)
RWKV_IN = sum(RWKV_SPLITS)
RWKV_IN_PAD = 4096
BRANCH_WIDTH = 1024
N_BRANCHES = 3
N_EXPERTS, TOP_K, EXPERT_FF = 32, 4, 512
SWIGLU_ALPHA, SWIGLU_LIMIT = 1.702, 7.0
DISPATCH_BLOCK = 128
DEEPNORM_ALPHA = (2 * DEPTH) ** 0.25
LN_EPS, RMS_EPS, GN_EPS = 1e-5, 1e-6, 64e-5

IN_SPLITS = (1024, 256, 256, MLA_Q_RANK, MLA_KV_RANK, MLA_ROPE_DIM, RWKV_IN, N_BRANCHES * D_MODEL)
SMALL_SPLITS = IN_SPLITS[:6]
SMALL_WIDTH = 3072
RWKV_COL0 = N_BRANCHES * D_MODEL
SMALL_COL0 = RWKV_COL0 + RWKV_IN_PAD

ROW_TILE = 256
LAT_TILES = SEQ // ROW_TILE
WKV_CHUNK = 64
WKV_PAIRS = RWKV_HEADS // 2
WKV_CHUNKS = N_TOK // WKV_CHUNK
WKV_CTX_CHUNKS = CTX_LEN // WKV_CHUNK
WKV_PREP_CHUNKS = 12
ATTN_KV_TILE = 768
TOKEN_SLAB = (8, 512)
EXPERT_SLAB = (8, 512)
VMEM_LIMIT = 48 * 1024 * 1024
MOE_VMEM_LIMIT = 56 * 1024 * 1024


def _params(*sem, vmem=VMEM_LIMIT):
    return pltpu.CompilerParams(dimension_semantics=sem, vmem_limit_bytes=vmem)


def _dot(a, b):
    return jnp.dot(a.astype(BF16), b.astype(BF16), preferred_element_type=F32)


def _dot_t(a, b):
    return lax.dot_general(a.astype(BF16), b.astype(BF16), (((1,), (1,)), ((), ())), preferred_element_type=F32)


def _tdot(a, b):
    return lax.dot_general(a.astype(BF16), b.astype(BF16), (((0,), (0,)), ((), ())), preferred_element_type=F32)


def _mm_kernel(a_ref, b_ref, bias_ref, o_ref):
    o_ref[...] = (_dot(a_ref[...], b_ref[...]) + bias_ref[...]).astype(o_ref.dtype)


def _pick(n, prefs):
    for p in prefs:
        if n % p == 0:
            return p
    return n


def matmul(a, b, bias=None, out_dtype=F32, tm=None, tn=None, layer=None, name="matmul"):
    m, kdim = a.shape
    n = b.shape[-1]
    tm = tm or _pick(m, (768, 512, 256, 128, 16))
    tn = tn or _pick(n, (512, 256, 128))
    if bias is None:
        bias = jnp.zeros((n,), F32)
    bias = bias.reshape(1, n).astype(F32)
    if layer is None:
        b_spec = pl.BlockSpec((kdim, tn), lambda i, j: (0, j))
    else:
        b_spec = pl.BlockSpec((None, kdim, tn), lambda i, j: (layer, 0, j))
    return pl.pallas_call(
        _mm_kernel,
        grid=(m // tm, n // tn),
        in_specs=[pl.BlockSpec((tm, kdim), lambda i, j: (i, 0)),
                  b_spec,
                  pl.BlockSpec((1, tn), lambda i, j: (0, j))],
        out_specs=pl.BlockSpec((tm, tn), lambda i, j: (i, j)),
        out_shape=jax.ShapeDtypeStruct((m, n), out_dtype),
        compiler_params=_params("parallel", "arbitrary"),
        name=name,
    )(a, b, bias)


def _router_kernel(x_ref, w_ref, b_ref, o_ref):
    x = x_ref[...]
    w = w_ref[...]
    xh = x.astype(BF16)
    xl = (x - xh.astype(F32)).astype(BF16)
    wh = w.astype(BF16)
    wl = (w - wh.astype(F32)).astype(BF16)
    acc = jnp.dot(xh, wh, preferred_element_type=F32)
    acc += jnp.dot(xh, wl, preferred_element_type=F32)
    acc += jnp.dot(xl, wh, preferred_element_type=F32)
    o_ref[...] = acc + b_ref[...]


def router_logits(tokens, router_w, router_b):
    n_tok = tokens.shape[0]
    npad = 128
    w = jnp.zeros((D_MODEL, npad), F32).at[:, :N_EXPERTS].set(router_w)
    b = jnp.zeros((1, npad), F32).at[0, :N_EXPERTS].set(router_b)
    out = pl.pallas_call(
        _router_kernel,
        grid=(n_tok // ROW_TILE,),
        in_specs=[pl.BlockSpec((ROW_TILE, D_MODEL), lambda i: (i, 0)),
                  pl.BlockSpec((D_MODEL, npad), lambda i: (0, 0)),
                  pl.BlockSpec((1, npad), lambda i: (0, 0))],
        out_specs=pl.BlockSpec((ROW_TILE, npad), lambda i: (i, 0)),
        out_shape=jax.ShapeDtypeStruct((n_tok, npad), F32),
        compiler_params=_params("parallel"),
        name="router",
    )(tokens, w, b)
    return out[:, :N_EXPERTS]


def _mod_index(i):
    return (i // LAT_TILES, 0, 0)


def _modulate_kernel(x_ref, sc_ref, sh_ref, o_ref):
    o_ref[...] = (x_ref[...] * (1.0 + sc_ref[0]) + sh_ref[0]).astype(o_ref.dtype)


def modulate(x, sc, sh, out_dtype=BF16):
    n = x.shape[0]
    return pl.pallas_call(
        _modulate_kernel,
        grid=(n // ROW_TILE,),
        in_specs=[pl.BlockSpec((ROW_TILE, D_MODEL), lambda i: (i, 0)),
                  pl.BlockSpec((1, 1, D_MODEL), _mod_index),
                  pl.BlockSpec((1, 1, D_MODEL), _mod_index)],
        out_specs=pl.BlockSpec((ROW_TILE, D_MODEL), lambda i: (i, 0)),
        out_shape=jax.ShapeDtypeStruct((n, D_MODEL), out_dtype),
        compiler_params=_params("parallel"),
        name="modulate",
    )(x, sc, sh)


def _layernorm(z, g, b):
    mu = jnp.mean(z, axis=-1, keepdims=True)
    zc = z - mu
    var = jnp.mean(zc * zc, axis=-1, keepdims=True)
    return zc * lax.rsqrt(var + LN_EPS) * g + b


def _postln_mod_kernel(x_ref, m_ref, g_ref, sc_ref, sh_ref, lng_ref, lnb_ref, y_ref, t_ref, tb_ref):
    y = _layernorm(DEEPNORM_ALPHA * x_ref[...] + g_ref[0] * m_ref[...], lng_ref[...], lnb_ref[...])
    y_ref[...] = y
    t = y * (1.0 + sc_ref[0]) + sh_ref[0]
    t_ref[...] = t
    for s in range(TOKEN_SLAB[0]):
        tb_ref[:, s, :] = t[:, s * TOKEN_SLAB[1]:(s + 1) * TOKEN_SLAB[1]]


def post_ln(x, m, g, ln_g, ln_b, sc, sh):
    n = x.shape[0]
    row = pl.BlockSpec((ROW_TILE, D_MODEL), lambda i: (i, 0))
    mod = pl.BlockSpec((1, 1, D_MODEL), _mod_index)
    vec = pl.BlockSpec((1, D_MODEL), lambda i: (0, 0))
    f32_rows = jax.ShapeDtypeStruct((n, D_MODEL), F32)
    return pl.pallas_call(
        _postln_mod_kernel,
        grid=(n // ROW_TILE,),
        in_specs=[row, row, mod, mod, mod, vec, vec],
        out_specs=[row, row, pl.BlockSpec((ROW_TILE,) + TOKEN_SLAB, lambda i: (i, 0, 0))],
        out_shape=[f32_rows, f32_rows, jax.ShapeDtypeStruct((n,) + TOKEN_SLAB, F32)],
        compiler_params=_params("parallel"),
        name="post_ln_mod",
    )(x, m, g, sc, sh, ln_g.reshape(1, D_MODEL), ln_b.reshape(1, D_MODEL))


def _attn_kernel(q_ref, k_ref, v_ref, o_ref, *, nkv, tk, unroll):
    g, tq, d = q_ref.shape[1:]
    dv = v_ref.shape[-1]
    rows = g * tq
    q = q_ref[0].reshape(rows, d)

    def body(j, carry):
        m_prev, l_prev, acc = carry
        start = pl.multiple_of(j * tk, tk)
        kj = k_ref[0, pl.ds(start, tk), :]
        vj = v_ref[0, pl.ds(start, tk), :]
        s = lax.dot_general(q, kj, (((1,), (1,)), ((), ())), preferred_element_type=F32)
        m_new = jnp.maximum(m_prev, jnp.max(s, axis=-1, keepdims=True))
        alpha = jnp.exp(m_prev - m_new)
        p = jnp.exp(s - m_new)
        l_new = alpha * l_prev + jnp.sum(p, axis=-1, keepdims=True)
        acc = alpha * acc + jnp.dot(p.astype(BF16), vj, preferred_element_type=F32)
        return m_new, l_new, acc

    init = (jnp.full((rows, 1), -jnp.inf, F32), jnp.zeros((rows, 1), F32), jnp.zeros((rows, dv), F32))
    _, l, acc = lax.fori_loop(0, nkv, body, init, unroll=unroll)
    o = acc / l
    for gi in range(g):
        o_ref[:, gi * dv:(gi + 1) * dv] = o[gi * tq:(gi + 1) * tq].astype(o_ref.dtype)


def attention(q, k, v, q_rows, kv_rows, tq, tk, unroll=2, name="attn"):
    hkv, g, _, d = q.shape
    dv = v.shape[-1]
    q0, nq = q_rows
    k0, nk = kv_rows
    assert q0 % tq == 0 and nq % tq == 0 and k0 % nk == 0 and nk % tk == 0
    qb, kb = q0 // tq, k0 // nk
    return pl.pallas_call(
        functools.partial(_attn_kernel, nkv=nk // tk, tk=tk, unroll=unroll),
        grid=(hkv, nq // tq),
        in_specs=[pl.BlockSpec((1, g, tq, d), lambda h, i: (h, 0, i + qb, 0)),
                  pl.BlockSpec((1, nk, d), lambda h, i: (h, kb, 0)),
                  pl.BlockSpec((1, nk, dv), lambda h, i: (h, kb, 0))],
        out_specs=pl.BlockSpec((tq, g * dv), lambda h, i: (i, h)),
        out_shape=jax.ShapeDtypeStruct((nq, hkv * g * dv), BF16),
        compiler_params=_params("parallel", "arbitrary"),
        name=name,
    )(q, k, v)


def lat_ctx_attention(q, k, v, tq, name):
    o_l = attention(q, k, v, (0, SEQ), (0, N_TOK), tq, ATTN_KV_TILE, unroll=N_TOK // ATTN_KV_TILE,
                    name=name + "_lat")
    o_c = attention(q, k, v, (SEQ, CTX_LEN), (SEQ, CTX_LEN), min(tq, CTX_LEN), CTX_LEN, unroll=1,
                    name=name + "_ctx")
    return jnp.concatenate([o_l, o_c], axis=0)


def _bd(x, m_left, m_right):
    return jnp.concatenate([x * m_left, x * m_right], axis=0)


def _cumsum_f32(tri, x):
    x1 = x.astype(BF16)
    r1 = x - x1.astype(F32)
    x2 = r1.astype(BF16)
    x3 = (r1 - x2.astype(F32)).astype(BF16)
    t = tri.astype(BF16)
    return (jnp.dot(t, x1, preferred_element_type=F32) + jnp.dot(t, x2, preferred_element_type=F32)
            + jnp.dot(t, x3, preferred_element_type=F32))


def _wkv_prep_kernel(r_ref, lw_ref, k_ref, v_ref, a_ref, b_ref,
                     wr_ref, u_ref, y0_ref, mrb_ref, bh_ref, g_ref, pc_ref, *, reverse):
    c = WKV_CHUNK
    n = RWKV_HEAD_DIM
    row = lax.broadcasted_iota(jnp.int32, (c, c), 0)
    col = lax.broadcasted_iota(jnp.int32, (c, c), 1)
    tri = ((col >= row) if reverse else (col <= row)).astype(F32)
    row2 = lax.broadcasted_iota(jnp.int32, (2 * c, 2 * c), 0) % c
    col2 = lax.broadcasted_iota(jnp.int32, (2 * c, 2 * c), 1) % c
    if reverse:
        incl, strict = col2 >= row2, col2 > row2
    else:
        incl, strict = col2 <= row2, col2 < row2
    lane = lax.broadcasted_iota(jnp.int32, (1, 2 * n), 1)
    m_left = (lane < n).astype(F32)
    m_right = 1.0 - m_left
    bd = functools.partial(_bd, m_left=m_left, m_right=m_right)
    last = 0 if reverse else c - 1

    chunks = range(WKV_PREP_CHUNKS)
    rows = [slice(ci * c, (ci + 1) * c) for ci in chunks]
    lw = [lw_ref[rw, :] for rw in rows]
    cum = [_cumsum_f32(tri, z) for z in lw]
    p_incl = [jnp.exp(z) for z in cum]
    p_inv = [jnp.exp(-z) for z in cum]
    pe = [z[last:last + 1, :] for z in p_incl]
    at = [bd(a_ref[rows[ci], :] * jnp.exp(cum[ci] - lw[ci])) for ci in chunks]
    rt = [bd(r_ref[rows[ci], :] * p_incl[ci]) for ci in chunks]
    bt = [bd(b_ref[rows[ci], :] * p_inv[ci]) for ci in chunks]
    kt = [bd(k_ref[rows[ci], :] * p_inv[ci]) for ci in chunks]
    v = [bd(v_ref[rows[ci], :]) for ci in chunks]
    big = [_dot_t(jnp.concatenate([at[ci], rt[ci]], axis=0), jnp.concatenate([bt[ci], kt[ci]], axis=0))
           for ci in chunks]
    l_pow = [jnp.where(strict, z[:2 * c, :2 * c], 0.0) for z in big]
    lm = [jnp.concatenate([jnp.where(strict, z[:2 * c, 2 * c:], 0.0),
                           jnp.where(incl, z[2 * c:, 2 * c:], 0.0)], axis=0) for z in big]
    for ci in chunks:
        wr_ref[ci, 0, 2 * c:, :] = rt[ci].astype(BF16)
        mrb_ref[ci, 0] = jnp.where(incl, big[ci][2 * c:, :2 * c], 0.0).astype(BF16)
        bh_ref[ci, 0] = (bt[ci] * pe[ci]).astype(BF16)
        pc_ref[ci, 0] = jnp.broadcast_to(pe[ci], (8, 2 * n))
    lv = [_dot(lm[ci], v[ci]) for ci in chunks]
    for ci in chunks:
        y0_ref[ci, 0] = lv[ci][2 * c:]
        g_ref[ci, 0] = _tdot(v[ci], kt[ci] * pe[ci])
    x = [jnp.concatenate([at[ci], lv[ci][:2 * c]], axis=1) for ci in chunks]
    x = [x[ci] + _dot(l_pow[ci], x[ci]) for ci in chunks]
    span = 1
    while span * 2 < c:
        l_pow = [_dot(z, z) for z in l_pow]
        x = [x[ci] + _dot(l_pow[ci], x[ci]) for ci in chunks]
        span *= 2
    for ci in chunks:
        wr_ref[ci, 0, :2 * c, :] = x[ci][:, :2 * n].astype(BF16)
        u_ref[ci, 0] = x[ci][:, 2 * n:]


def _wkv_chunk_fwd(s):
    return (s + SEQ // WKV_CHUNK) % WKV_CHUNKS


def _wkv_chunk_bwd(s):
    return WKV_CHUNKS - 1 - s


def wkv_prep(r, lw, k, v, a, b, reverse):
    c, n2 = WKV_CHUNK, 2 * RWKV_HEAD_DIM
    cb = WKV_PREP_CHUNKS
    spec = pl.BlockSpec((cb * c, n2), lambda p, i: (i, p))

    def out(rows, dtype):
        return (pl.BlockSpec((cb, 1, rows, n2), lambda p, i: (i, p, 0, 0)),
                jax.ShapeDtypeStruct((WKV_CHUNKS, WKV_PAIRS, rows, n2), dtype))

    outs = [out(4 * c, BF16), out(2 * c, F32), out(2 * c, F32), out(2 * c, BF16), out(2 * c, BF16),
            out(2 * c, F32), out(8, F32)]
    return pl.pallas_call(
        functools.partial(_wkv_prep_kernel, reverse=reverse),
        grid=(WKV_PAIRS, WKV_CHUNKS // cb),
        in_specs=[spec] * 6,
        out_specs=[o[0] for o in outs],
        out_shape=[o[1] for o in outs],
        compiler_params=_params("parallel", "parallel"),
        name="wkv_prep_bwd" if reverse else "wkv_prep_fwd",
    )(r, lw, k, v, a, b)


def _wkv_scan_kernel(*refs):
    c, n = WKV_CHUNK, RWKV_HEAD_DIM
    ins, (yf_ref, yb_ref, s_ref) = refs[:14], refs[14:]

    @pl.when(pl.program_id(0) == 0)
    def _():
        s_ref[...] = jnp.zeros_like(s_ref)

    jobs = [(d, p) for d in range(2) for p in range(WKV_PAIRS)]
    ref = lambda d, k: ins[7 * d + k]
    s0 = [s_ref[d, p] for d, p in jobs]
    z = [_dot_t(ref(d, 0)[0, p], s0[j]) for j, (d, p) in enumerate(jobs)]
    sa = [z[j][:2 * c] + ref(d, 1)[0, p] for j, (d, p) in enumerate(jobs)]
    for j, (d, p) in enumerate(jobs):
        s_ref[d, p] = s0[j] * ref(d, 6)[0, p, 0:1, :] + ref(d, 5)[0, p] + _tdot(sa[j], ref(d, 4)[0, p])
    for j, (d, p) in enumerate(jobs):
        y = z[j][2 * c:] + ref(d, 2)[0, p] + _dot(ref(d, 3)[0, p], sa[j])
        (yf_ref, yb_ref)[d][:, p * 2 * n:(p + 1) * 2 * n] = y[:c] + y[c:]


def wkv_scan(prep_f, prep_b):
    c, n2 = WKV_CHUNK, 2 * RWKV_HEAD_DIM

    def specs(chunk_of):
        return [pl.BlockSpec((1, WKV_PAIRS) + a.shape[2:], lambda s: (chunk_of(s), 0, 0, 0)) for a in prep_f]

    y_shape = jax.ShapeDtypeStruct((N_TOK, RWKV_WIDTH), F32)
    return pl.pallas_call(
        _wkv_scan_kernel,
        grid=(WKV_CHUNKS,),
        in_specs=specs(_wkv_chunk_fwd) + specs(_wkv_chunk_bwd),
        out_specs=[pl.BlockSpec((c, RWKV_WIDTH), lambda s: (_wkv_chunk_fwd(s), 0)),
                   pl.BlockSpec((c, RWKV_WIDTH), lambda s: (_wkv_chunk_bwd(s), 0))],
        out_shape=[y_shape, y_shape],
        scratch_shapes=[pltpu.VMEM((2, WKV_PAIRS, n2, n2), F32)],
        compiler_params=_params("arbitrary"),
        name="wkv_scan",
    )(*prep_f, *prep_b)


MOE_CAST_ROWS = 256


def _moe_kernel(be_ref, nx_ref, idx_ref, idx_next_ref, tok_hbm, w1_hbm, b1_ref, w2_hbm, b2_ref, g_ref, o_ref,
                w1f_ref, w2f_ref, w1b_ref, w2b_ref, xbuf, sem, wsem, *, layer):
    i = pl.program_id(0)
    cur = i % 2

    def weight_copies(e):
        return (pltpu.make_async_copy(w1_hbm.at[layer, e], w1f_ref, wsem.at[0]),
                pltpu.make_async_copy(w2_hbm.at[layer, e], w2f_ref, wsem.at[1]))

    @pl.when(i == 0)
    def _():
        for cp in weight_copies(be_ref[0]):
            cp.start()

    def row_copy(idx, r, buf):
        return pltpu.make_async_copy(tok_hbm.at[idx[0, 0, r]], xbuf.at[buf, r], sem.at[buf])

    def start_rows(idx, buf):
        def body(r, carry):
            row_copy(idx, r, buf).start()
            return carry
        lax.fori_loop(0, DISPATCH_BLOCK, body, 0, unroll=8)

    @pl.when(i == 0)
    def _():
        start_rows(idx_ref, 0)

    @pl.when(i + 1 < pl.num_programs(0))
    def _():
        start_rows(idx_next_ref, 1 - cur)

    new_expert = (i == 0) | (be_ref[i] != be_ref[jnp.maximum(i - 1, 0)])

    @pl.when(new_expert)
    def _():
        for cp in weight_copies(be_ref[i]):
            cp.wait()

        def cast1(c, carry):
            rows = pl.ds(pl.multiple_of(c * MOE_CAST_ROWS, MOE_CAST_ROWS), MOE_CAST_ROWS)
            w1b_ref[rows, :] = w1f_ref[rows, :].astype(BF16)
            return carry

        def cast2(c, carry):
            rows = pl.ds(pl.multiple_of(c * MOE_CAST_ROWS, MOE_CAST_ROWS), MOE_CAST_ROWS)
            w2b_ref[rows, :] = w2f_ref[rows, :].astype(BF16)
            return carry

        lax.fori_loop(0, D_MODEL // MOE_CAST_ROWS, cast1, 0)
        lax.fori_loop(0, EXPERT_FF // MOE_CAST_ROWS, cast2, 0)

        @pl.when(nx_ref[i] >= 0)
        def _():
            for cp in weight_copies(nx_ref[i]):
                cp.start()

    def wait_row(r, carry):
        row_copy(idx_ref, r, cur).wait()
        return carry
    lax.fori_loop(0, DISPATCH_BLOCK, wait_row, 0, unroll=8)

    x = xbuf[cur].reshape(DISPATCH_BLOCK, D_MODEL).astype(BF16)
    u = jnp.dot(x, w1b_ref[...], preferred_element_type=F32) + b1_ref[0]
    u_glu = jnp.minimum(u[:, :EXPERT_FF], SWIGLU_LIMIT)
    u_lin = jnp.clip(u[:, EXPERT_FF:], -SWIGLU_LIMIT, SWIGLU_LIMIT)
    act = (u_glu * jax.nn.sigmoid(SWIGLU_ALPHA * u_glu) * (u_lin + 1.0)).astype(BF16)
    gate = g_ref[...]
    ew = EXPERT_SLAB[1]
    for s in range(EXPERT_SLAB[0]):
        cols = slice(s * ew, (s + 1) * ew)
        y = jnp.dot(act, w2b_ref[:, cols], preferred_element_type=F32) + b2_ref[0, :, cols]
        o_ref[:, s, :] = y * gate


def moe_experts(token_slabs, slot_token, slot_gate, block_expert, w1, b1, w2, b2, layer):
    n_slots = slot_token.shape[0]
    n_blocks = n_slots // DISPATCH_BLOCK
    idx = slot_token.reshape(n_blocks, 1, DISPATCH_BLOCK)
    idx_spec = lambda nxt: pl.BlockSpec((1, 1, DISPATCH_BLOCK),
                                        lambda i, be, nx: (jnp.minimum(i + nxt, n_blocks - 1), 0, 0),
                                        memory_space=pltpu.SMEM)
    later = jnp.where(block_expert[None, :] > block_expert[:, None], block_expert[None, :], N_EXPERTS)
    next_expert = jnp.min(later, axis=1)
    next_expert = jnp.where(next_expert == N_EXPERTS, -1, next_expert).astype(jnp.int32)
    grid_spec = pltpu.PrefetchScalarGridSpec(
        num_scalar_prefetch=2,
        grid=(n_blocks,),
        in_specs=[idx_spec(0), idx_spec(1),
                  pl.BlockSpec(memory_space=pl.ANY),
                  pl.BlockSpec(memory_space=pl.ANY),
                  pl.BlockSpec((1, 1, 2 * EXPERT_FF), lambda i, be, nx: (be[i], 0, 0)),
                  pl.BlockSpec(memory_space=pl.ANY),
                  pl.BlockSpec((1, 1, D_MODEL), lambda i, be, nx: (be[i], 0, 0)),
                  pl.BlockSpec((DISPATCH_BLOCK, 1), lambda i, be, nx: (i, 0))],
        out_specs=pl.BlockSpec((DISPATCH_BLOCK,) + EXPERT_SLAB, lambda i, be, nx: (i, 0, 0)),
        scratch_shapes=[pltpu.VMEM((D_MODEL, 2 * EXPERT_FF), F32), pltpu.VMEM((EXPERT_FF, D_MODEL), F32),
                        pltpu.VMEM((D_MODEL, 2 * EXPERT_FF), BF16), pltpu.VMEM((EXPERT_FF, D_MODEL), BF16),
                        pltpu.VMEM((2, DISPATCH_BLOCK) + TOKEN_SLAB, F32), pltpu.SemaphoreType.DMA((2,)),
                        pltpu.SemaphoreType.DMA((2,))],
    )
    return pl.pallas_call(
        functools.partial(_moe_kernel, layer=layer),
        grid_spec=grid_spec,
        out_shape=jax.ShapeDtypeStruct((n_slots,) + EXPERT_SLAB, F32),
        compiler_params=_params("arbitrary", vmem=MOE_VMEM_LIMIT),
        name="moe_experts",
    )(block_expert, next_expert, idx, idx, token_slabs, w1, b1.reshape(N_EXPERTS, 1, -1), w2,
      b2.reshape(N_EXPERTS, 1, -1), slot_gate.reshape(n_slots, 1))


COMBINE_ROWS = 64


def _combine_ln_kernel(idx_ref, idx_next_ref, y_hbm, x_ref, g_ref, lng_ref, lnb_ref, o_ref, buf, sem):
    i = pl.program_id(0)
    cur = i % 2

    def row_copy(idx, r, j, b):
        return pltpu.make_async_copy(y_hbm.at[idx[0, 0, r * TOP_K + j]], buf.at[b, j, r], sem.at[b])

    def start_rows(idx, b):
        def body(r, carry):
            for j in range(TOP_K):
                row_copy(idx, r, j, b).start()
            return carry
        lax.fori_loop(0, COMBINE_ROWS, body, 0, unroll=4)

    @pl.when(i == 0)
    def _():
        start_rows(idx_ref, 0)

    @pl.when(i + 1 < pl.num_programs(0))
    def _():
        start_rows(idx_next_ref, 1 - cur)

    def wait_rows(r, carry):
        for j in range(TOP_K):
            row_copy(idx_ref, r, j, cur).wait()
        return carry
    lax.fori_loop(0, COMBINE_ROWS, wait_rows, 0, unroll=4)

    f = buf[cur, 0]
    for j in range(1, TOP_K):
        f = f + buf[cur, j]
    f = f.reshape(COMBINE_ROWS, D_MODEL)
    o_ref[...] = _layernorm(DEEPNORM_ALPHA * x_ref[...] + g_ref[0] * f, lng_ref[...], lnb_ref[...])


def moe_combine_ln(y, slot, x, g, ln_g, ln_b):
    n = x.shape[0]
    nb = n // COMBINE_ROWS
    idx = slot.reshape(nb, 1, COMBINE_ROWS * TOP_K)
    idx_spec = lambda nxt: pl.BlockSpec((1, 1, COMBINE_ROWS * TOP_K),
                                        lambda i: (jnp.minimum(i + nxt, nb - 1), 0, 0),
                                        memory_space=pltpu.SMEM)
    row = pl.BlockSpec((COMBINE_ROWS, D_MODEL), lambda i: (i, 0))
    mod = pl.BlockSpec((1, 1, D_MODEL), lambda i: (i // (SEQ // COMBINE_ROWS), 0, 0))
    vec = pl.BlockSpec((1, D_MODEL), lambda i: (0, 0))
    return pl.pallas_call(
        _combine_ln_kernel,
        grid=(nb,),
        in_specs=[idx_spec(0), idx_spec(1), pl.BlockSpec(memory_space=pl.ANY), row, mod, vec, vec],
        out_specs=row,
        out_shape=jax.ShapeDtypeStruct((n, D_MODEL), F32),
        scratch_shapes=[pltpu.VMEM((2, TOP_K, COMBINE_ROWS) + EXPERT_SLAB, F32), pltpu.SemaphoreType.DMA((2,))],
        compiler_params=_params("arbitrary"),
        name="moe_combine_ln",
    )(idx, idx, y, x, g, ln_g.reshape(1, D_MODEL), ln_b.reshape(1, D_MODEL))


def moe(tokens, token_slabs, router_w, router_b, w1, b1, w2, b2, layer):
    n_tok = tokens.shape[0]
    n_assign = n_tok * TOP_K
    n_blocks = -(-(n_assign + N_EXPERTS * (DISPATCH_BLOCK - 1)) // DISPATCH_BLOCK)
    n_slots = n_blocks * DISPATCH_BLOCK
    logits = router_logits(tokens, router_w, router_b)
    top_val, top_idx = lax.top_k(logits, TOP_K)
    gate = jax.nn.softmax(top_val, axis=-1).reshape(-1)
    expert = top_idx.reshape(-1)
    onehot = (expert[:, None] == jnp.arange(N_EXPERTS, dtype=expert.dtype)[None, :]).astype(jnp.int32)
    before = jnp.cumsum(onehot, axis=0) - onehot
    counts = before[-1] + onehot[-1]
    padded = (counts + DISPATCH_BLOCK - 1) // DISPATCH_BLOCK * DISPATCH_BLOCK
    pad_end = jnp.cumsum(padded)
    pad_start = pad_end - padded
    slot = jnp.sum(onehot * (before + pad_start[None, :]), axis=1).astype(jnp.int32)
    slot_assign = jnp.full((n_slots,), -1, jnp.int32).at[slot].set(jnp.arange(n_assign, dtype=jnp.int32))
    filled = slot_assign >= 0
    src = jnp.maximum(slot_assign, 0)
    slot_token = jnp.where(filled, src // TOP_K, 0)
    slot_gate = jnp.where(filled, gate[src], 0.0)
    block_start = jnp.arange(n_blocks, dtype=pad_end.dtype) * DISPATCH_BLOCK
    block_expert = jnp.minimum(jnp.sum(pad_end[None, :] <= block_start[:, None], axis=1),
                               N_EXPERTS - 1).astype(jnp.int32)
    y = moe_experts(token_slabs, slot_token, slot_gate, block_expert, w1, b1, w2, b2, layer)
    return y, slot.reshape(n_tok, TOP_K)


def _rope_lanes(x, cos, sin, half):
    lane = lax.broadcasted_iota(jnp.int32, x.shape, 1)
    up = pltpu.roll(x, 128 - half, 1)
    down = pltpu.roll(x, half, 1)
    return x * cos + jnp.where(lane % (2 * half) < half, up, down) * sin


def _rms(x, g):
    return x * lax.rsqrt(jnp.mean(x * x, axis=-1, keepdims=True) + RMS_EPS) * g


def _attn_prep_kernel(pa_ref, pb_ref, pc_ref, gq_ref, gk_ref, gmq_ref, gmkv_ref, cosa_ref, sina_ref,
                      cosb_ref, sinb_ref, wuq_ref, wukv_ref, qg_ref, kg_ref, vg_ref, qc_ref, kc_ref, vm_ref):
    hd = GQA_HEAD_DIM
    ca, sa, cb, sb = cosa_ref[...], sina_ref[...], cosb_ref[...], sinb_ref[...]
    grp = GQA_HEADS // GQA_KV_HEADS
    for h in range(GQA_HEADS):
        q = _rope_lanes(_rms(pa_ref[:, h * hd:(h + 1) * hd], gq_ref[...]), ca, sa, hd // 4)
        qg_ref[h // grp, h % grp] = (q * GQA_HEAD_DIM ** -0.5).astype(BF16)
    for h in range(GQA_KV_HEADS):
        k = _rope_lanes(_rms(pb_ref[:, h * hd:(h + 1) * hd], gk_ref[...]), ca, sa, hd // 4)
        kg_ref[h] = k.astype(BF16)
        vg_ref[h] = pb_ref[:, (GQA_KV_HEADS + h) * hd:(GQA_KV_HEADS + h + 1) * hd].astype(BF16)
    dkvn = _rms(pb_ref[:, 2 * GQA_KV_HEADS * hd:], gmkv_ref[...])
    dqn = _rms(pc_ref[:, :MLA_Q_RANK], gmq_ref[...])
    k_rope = _rope_lanes(pc_ref[:, MLA_Q_RANK:], cb, sb, MLA_ROPE_DIM // 4)
    qb = _dot(dqn, wuq_ref[...])
    kvb = _dot(dkvn, wukv_ref[...])
    scale = MLA_QK_DIM ** -0.5
    for h in range(MLA_HEADS):
        base = h * MLA_CAT_DIM
        q_nope = qb[:, base:base + 128]
        q_rope = _rope_lanes(qb[:, base + 128:base + 256], cb, sb, MLA_ROPE_DIM // 4)
        qc_ref[h, 0] = (jnp.concatenate([q_nope, q_rope], axis=1) * scale).astype(BF16)
        kc_ref[h] = jnp.concatenate([kvb[:, base:base + 128], k_rope], axis=1).astype(BF16)
        vm_ref[h] = kvb[:, base + 128:base + 256].astype(BF16)


def _rope_lane_tables(rot_dim):
    n_freq = rot_dim // 4
    row = jnp.repeat(jnp.arange(SEQ // GRID_W, dtype=F32), GRID_W)
    col = (jnp.arange(SEQ) % GRID_W).astype(F32)
    freqs = ROPE_THETA ** (-jnp.arange(n_freq, dtype=F32) / n_freq)
    ar, ac = row[:, None] * freqs, col[:, None] * freqs
    cos = jnp.concatenate([jnp.cos(ar), jnp.cos(ar), jnp.cos(ac), jnp.cos(ac)], axis=1)
    sin = jnp.concatenate([-jnp.sin(ar), jnp.sin(ar), -jnp.sin(ac), jnp.sin(ac)], axis=1)
    cos = jnp.pad(cos, ((0, CTX_LEN), (0, 128 - rot_dim)), constant_values=1.0)
    sin = jnp.pad(sin, ((0, CTX_LEN), (0, 128 - rot_dim)))
    return cos, sin


def attn_prep(proj, gqa_q_norm, gqa_k_norm, mla_q_norm, mla_kv_norm, mla_w_uq, mla_w_ukv):
    t = N_TOK
    cos_a, sin_a = _rope_lane_tables(GQA_HEAD_DIM)
    cos_b, sin_b = _rope_lane_tables(MLA_ROPE_DIM)
    w_uq = jnp.pad(mla_w_uq.reshape(MLA_Q_RANK, MLA_HEADS, MLA_QK_DIM),
                   ((0, 0), (0, 0), (0, MLA_CAT_DIM - MLA_QK_DIM))).reshape(MLA_Q_RANK, -1).astype(BF16)
    w_ukv = mla_w_ukv.astype(BF16)
    first = SMALL_COL0 // 1024
    col = lambda j: pl.BlockSpec((ROW_TILE, 1024), lambda i: (i, first + j))
    vec = lambda n: pl.BlockSpec((1, n), lambda i: (0, 0))
    tab = pl.BlockSpec((ROW_TILE, 128), lambda i: (i, 0))
    full = lambda a: pl.BlockSpec(a.shape, lambda i: (0, 0))
    grp = GQA_HEADS // GQA_KV_HEADS
    hd = GQA_HEAD_DIM
    outs = [
        (pl.BlockSpec((GQA_KV_HEADS, grp, ROW_TILE, hd), lambda i: (0, 0, i, 0)), (GQA_KV_HEADS, grp, t, hd)),
        (pl.BlockSpec((GQA_KV_HEADS, ROW_TILE, hd), lambda i: (0, i, 0)), (GQA_KV_HEADS, t, hd)),
        (pl.BlockSpec((GQA_KV_HEADS, ROW_TILE, hd), lambda i: (0, i, 0)), (GQA_KV_HEADS, t, hd)),
        (pl.BlockSpec((MLA_HEADS, 1, ROW_TILE, MLA_CAT_DIM), lambda i: (0, 0, i, 0)), (MLA_HEADS, 1, t, MLA_CAT_DIM)),
        (pl.BlockSpec((MLA_HEADS, ROW_TILE, MLA_CAT_DIM), lambda i: (0, i, 0)), (MLA_HEADS, t, MLA_CAT_DIM)),
        (pl.BlockSpec((MLA_HEADS, ROW_TILE, MLA_V_DIM), lambda i: (0, i, 0)), (MLA_HEADS, t, MLA_V_DIM)),
    ]
    return pl.pallas_call(
        _attn_prep_kernel,
        grid=(t // ROW_TILE,),
        in_specs=[col(0), col(1), col(2), vec(hd), vec(hd), vec(MLA_Q_RANK), vec(MLA_KV_RANK),
                  tab, tab, tab, tab, full(w_uq), full(w_ukv)],
        out_specs=[o[0] for o in outs],
        out_shape=[jax.ShapeDtypeStruct(o[1], BF16) for o in outs],
        compiler_params=_params("parallel"),
        name="attn_prep",
    )(proj, proj, proj, gqa_q_norm.reshape(1, -1), gqa_k_norm.reshape(1, -1), mla_q_norm.reshape(1, -1),
      mla_kv_norm.reshape(1, -1), cos_a, sin_a, cos_b, sin_b, w_uq, w_ukv)


def _group_sum(x, ones_bd):
    hi = x.astype(BF16)
    lo = (x - hi.astype(F32)).astype(BF16)
    return jnp.dot(hi, ones_bd, preferred_element_type=F32) + jnp.dot(lo, ones_bd, preferred_element_type=F32)


def _softplus(z):
    return jnp.maximum(z, 0.0) + jnp.log(1.0 + jnp.exp(-jnp.abs(z)))


def _rwkv_prep_kernel(p_ref, prev_ref, next_ref, mu_ref, wup_ref, w0_ref, aup_ref, a0_ref, gup_ref,
                      kk_ref, ka_ref, rk_ref, ones_ref,
                      r_ref, v_ref, a_ref, lwf_ref, kdf_ref, bf_ref, lwb_ref, kdb_ref, bb_ref, g_ref, bonus_ref):
    i = pl.program_id(0)
    w = RWKV_WIDTH
    p = p_ref[...]
    n = p.shape[0]
    rowid = lax.broadcasted_iota(jnp.int32, (n, 1), 0)
    has_prev = jnp.where(i % LAT_TILES == 0, 0.0, 1.0)
    has_next = jnp.where((i == LAT_TILES - 1) | (i == LAT_TILES), 0.0, 1.0)
    prev = jnp.where(rowid == 0, prev_ref[7:8, :] * has_prev, pltpu.roll(p, 1, 0))
    nxt = jnp.where(rowid == n - 1, next_ref[0:1, :] * has_next, pltpu.roll(p, n - 1, 0))
    p = p + mu_ref[0:1, :] * (prev - p) + mu_ref[1:2, :] * (nxt - p)

    ones_bd = ones_ref[...]
    r, kx, vx = p[:, :w], p[:, w:2 * w], p[:, 2 * w:3 * w]
    low = p[:, 3 * w:]
    g_ref[...] = _dot(jax.nn.sigmoid(low[:, 512:]), gup_ref[...])
    kk = kx * kk_ref[...]
    kk = kk / jnp.maximum(jnp.sqrt(_group_sum(kk * kk, ones_bd)), 1e-12)
    r_ref[...] = r
    v_ref[...] = vx
    a_ref[...] = -kk
    k_sum = jnp.zeros_like(kx)
    for d, (lw_ref, kd_ref, b_ref) in enumerate(((lwf_ref, kdf_ref, bf_ref), (lwb_ref, kdb_ref, bb_ref))):
        wd = low[:, d * 128:(d + 1) * 128]
        ad = low[:, 256 + d * 128:256 + (d + 1) * 128]
        w_lin = _dot(jnp.tanh(wd), wup_ref[d]) + w0_ref[d:d + 1, :]
        lw_ref[...] = -jnp.exp(-_softplus(-w_lin) - 0.5)
        a_gate = jax.nn.sigmoid(_dot(ad, aup_ref[d]) + a0_ref[d:d + 1, :])
        k_d = kx * (1.0 + (a_gate - 1.0) * ka_ref[...])
        kd_ref[...] = k_d
        b_ref[...] = kk * a_gate
        k_sum = k_sum + k_d
    bonus_ref[...] = _group_sum(r * k_sum * rk_ref[...], ones_bd) * vx


def _head_ones():
    h = jnp.arange(RWKV_WIDTH) // RWKV_HEAD_DIM
    return (h[:, None] == h[None, :]).astype(BF16)


def rwkv_prep(proj, rwkv_mu, rwkv_w0, rwkv_w_up, rwkv_a0, rwkv_a_up, rwkv_g_up, rwkv_k_k, rwkv_k_a, rwkv_r_k):
    t, w = N_TOK, RWKV_WIDTH
    cb = RWKV_COL0 // RWKV_IN_PAD
    n8 = t // 8
    per8 = ROW_TILE // 8
    mu = jnp.pad(rwkv_mu, ((0, 6), (0, RWKV_IN_PAD - RWKV_IN)))
    g_up = jnp.pad(rwkv_g_up, ((0, 512 - RWKV_GATE_RANK), (0, 0))).astype(BF16)
    full = lambda a: pl.BlockSpec(a.shape, lambda i: (0,) * a.ndim)
    consts = [mu, rwkv_w_up.astype(BF16), rwkv_w0, rwkv_a_up.astype(BF16), rwkv_a0, g_up,
              rwkv_k_k.reshape(1, w), rwkv_k_a.reshape(1, w), rwkv_r_k.reshape(1, w), _head_ones()]
    out_spec = pl.BlockSpec((ROW_TILE, w), lambda i: (i, 0))
    return pl.pallas_call(
        _rwkv_prep_kernel,
        grid=(t // ROW_TILE,),
        in_specs=[pl.BlockSpec((ROW_TILE, RWKV_IN_PAD), lambda i: (i, cb)),
                  pl.BlockSpec((8, RWKV_IN_PAD), lambda i: (jnp.maximum(i * per8 - 1, 0), cb)),
                  pl.BlockSpec((8, RWKV_IN_PAD), lambda i: (jnp.minimum((i + 1) * per8, n8 - 1), cb))]
                 + [full(a) for a in consts],
        out_specs=[out_spec] * 11,
        out_shape=[jax.ShapeDtypeStruct((t, w), F32)] * 11,
        compiler_params=_params("parallel"),
        name="rwkv_prep",
    )(proj, proj, proj, *consts)


def _rwkv_out_kernel(yf_ref, yb_ref, bonus_ref, g_ref, gng_ref, gnb_ref, ones_ref, o_ref):
    ones_bd = ones_ref[...]
    y = yf_ref[...] + yb_ref[...]
    yc = y - _group_sum(y, ones_bd) * (1.0 / RWKV_HEAD_DIM)
    var = _group_sum(yc * yc, ones_bd) * (1.0 / RWKV_HEAD_DIM)
    yn = yc * lax.rsqrt(var + GN_EPS) * gng_ref[...] + gnb_ref[...]
    o_ref[...] = ((yn + bonus_ref[...]) * g_ref[...]).astype(o_ref.dtype)


def rwkv_out(y_f, y_b, bonus, g, gn_g, gn_b):
    t, w = N_TOK, RWKV_WIDTH
    row = pl.BlockSpec((ROW_TILE, w), lambda i: (i, 0))
    vec = pl.BlockSpec((1, w), lambda i: (0, 0))
    ones = _head_ones()
    return pl.pallas_call(
        _rwkv_out_kernel,
        grid=(t // ROW_TILE,),
        in_specs=[row, row, row, row, vec, vec, pl.BlockSpec(ones.shape, lambda i: (0, 0))],
        out_specs=row,
        out_shape=jax.ShapeDtypeStruct((t, w), BF16),
        compiler_params=_params("parallel"),
        name="rwkv_out",
    )(y_f, y_b, bonus, g, gn_g.reshape(1, w), gn_b.reshape(1, w), ones)


def _merge_kernel(oa_ref, ob_ref, oc_ref, wa_ref, wb_ref, wc_ref, ga_ref, gb_ref, gc_ref, m_ref):
    acc = jax.nn.sigmoid(ga_ref[...]) * _dot(oa_ref[...], wa_ref[0])
    acc += jax.nn.sigmoid(gb_ref[...]) * _dot(ob_ref[...], wb_ref[0])
    acc += jax.nn.sigmoid(gc_ref[...]) * _dot(oc_ref[...], wc_ref[0])
    m_ref[...] = acc.astype(m_ref.dtype)


def merge(oa, ob, oc, proj, w_branch_up):
    t = N_TOK
    tm, tn = 768, 512
    nj = D_MODEL // tn
    wb = w_branch_up.astype(BF16)
    o_spec = pl.BlockSpec((tm, BRANCH_WIDTH), lambda i, j: (i, 0))
    w_spec = lambda b: pl.BlockSpec((1, BRANCH_WIDTH, tn), lambda i, j: (b, 0, j))
    g_spec = lambda b: pl.BlockSpec((tm, tn), lambda i, j: (i, b * nj + j))
    return pl.pallas_call(
        _merge_kernel,
        grid=(t // tm, nj),
        in_specs=[o_spec, o_spec, o_spec, w_spec(0), w_spec(1), w_spec(2), g_spec(0), g_spec(1), g_spec(2)],
        out_specs=pl.BlockSpec((tm, tn), lambda i, j: (i, j)),
        out_shape=jax.ShapeDtypeStruct((t, D_MODEL), BF16),
        compiler_params=_params("parallel", "arbitrary"),
        name="merge",
    )(oa, ob, oc, wb, wb, wb, proj, proj, proj)


def _split(x, sizes):
    return jnp.split(x, np.cumsum(sizes)[:-1].tolist(), axis=-1)


def _in_proj_weight(w_in):
    aq, ak, av, dq, dkv, kr, rwkv, gates = _split(w_in, IN_SPLITS)
    small = jnp.concatenate([aq, ak, av, dkv, dq, kr], axis=1)
    pad = lambda z, n: jnp.pad(z, ((0, 0), (0, n - z.shape[1])))
    return jnp.concatenate([gates, pad(rwkv, RWKV_IN_PAD), pad(small, SMALL_WIDTH)], axis=1).astype(BF16)


def _mixer(h, w_in, gqa_q_norm, gqa_k_norm, mla_q_norm, mla_kv_norm, mla_w_uq, mla_w_ukv,
           rwkv_mu, rwkv_w0, rwkv_w_up, rwkv_a0, rwkv_a_up, rwkv_g_up, rwkv_k_k, rwkv_k_a, rwkv_r_k,
           rwkv_gn_g, rwkv_gn_b, w_branch_up, w_o):
    proj = matmul(h, _in_proj_weight(w_in), name="in_proj")

    qg, kg, vg, qc, kc, vm = attn_prep(proj, gqa_q_norm, gqa_k_norm, mla_q_norm, mla_kv_norm, mla_w_uq, mla_w_ukv)
    oa = lat_ctx_attention(qg, kg, vg, 256, "gqa")
    ob = lat_ctx_attention(qc, kc, vm, 1024, "mla")

    r, v, a, lw_f, kd_f, b_f, lw_b, kd_b, b_b, g, bonus = rwkv_prep(
        proj, rwkv_mu, rwkv_w0, rwkv_w_up, rwkv_a0, rwkv_a_up, rwkv_g_up, rwkv_k_k, rwkv_k_a, rwkv_r_k)
    y_f, y_b = wkv_scan(wkv_prep(r, lw_f, kd_f, v, a, b_f, reverse=False),
                        wkv_prep(r, lw_b, kd_b, v, a, b_b, reverse=True))
    oc = rwkv_out(y_f, y_b, bonus, g, rwkv_gn_g, rwkv_gn_b)

    m = merge(oa, ob, oc, proj, w_branch_up)
    return matmul(m, w_o.astype(BF16), name="out_proj")


def kernel(x, c, ctx, c_ctx, w_ada, b_ada, w_in, gqa_q_norm, gqa_k_norm, mla_q_norm, mla_kv_norm, mla_w_uq,
           mla_w_ukv, rwkv_mu, rwkv_w0, rwkv_w_up, rwkv_a0, rwkv_a_up, rwkv_g_up, rwkv_k_k, rwkv_k_a, rwkv_r_k,
           rwkv_gn_g, rwkv_gn_b, w_branch_up, w_o, ln1_g, ln1_b, router_w, router_b, exp_w_in, exp_b_in,
           exp_w_out, exp_b_out, ln2_g, ln2_b):
    xs = jnp.concatenate([x[0], ctx[0]], axis=0)
    cond = jnp.zeros((16, D_MODEL), F32).at[0].set(jax.nn.silu(c[0])).at[1].set(jax.nn.silu(c_ctx))
    for l in range(DEPTH):
        ada = matmul(cond.astype(BF16), w_ada, bias=b_ada[l], tm=16, tn=512, layer=l, name="adaln")
        sh1, sc1, g1, sh2, sc2, g2 = (z[:2].reshape(2, 1, D_MODEL) for z in _split(ada, (D_MODEL,) * 6))
        h = modulate(xs, sc1, sh1)
        m = _mixer(h, w_in[l], gqa_q_norm[l], gqa_k_norm[l], mla_q_norm[l], mla_kv_norm[l], mla_w_uq[l],
                   mla_w_ukv[l], rwkv_mu[l], rwkv_w0[l], rwkv_w_up[l], rwkv_a0[l], rwkv_a_up[l], rwkv_g_up[l],
                   rwkv_k_k[l], rwkv_k_a[l], rwkv_r_k[l], rwkv_gn_g[l], rwkv_gn_b[l], w_branch_up[l], w_o[l])
        xs, tokens, token_slabs = post_ln(xs, m, g1, ln1_g[l], ln1_b[l], sc2, sh2)
        y, slot = moe(tokens, token_slabs, router_w[l], router_b[l], exp_w_in, exp_b_in[l], exp_w_out,
                      exp_b_out[l], l)
        xs = moe_combine_ln(y, slot, xs, g2, ln2_g[l], ln2_b[l])
    return xs[:SEQ].reshape(1, SEQ, D_MODEL)
```

```python
import functools

import jax
import jax.numpy as jnp
import numpy as np
from jax import lax
from jax.experimental import pallas as pl
from jax.experimental.pallas import tpu as pltpu

F32, BF16 = jnp.float32, jnp.bfloat16

D_MODEL = 4096
SEQ = 8192
CTX_LEN = 256
N_TOK = SEQ + CTX_LEN
DEPTH = 2
GRID_W = 64
ROPE_THETA = 10000.0

GQA_HEADS, GQA_KV_HEADS, GQA_HEAD_DIM = 8, 2, 128
MLA_HEADS, MLA_Q_RANK, MLA_KV_RANK = 8, 896, 512
MLA_NOPE_DIM, MLA_ROPE_DIM, MLA_V_DIM = 128, 64, 128
MLA_QK_DIM = MLA_NOPE_DIM + MLA_ROPE_DIM
MLA_CAT_DIM = 256
RWKV_HEADS, RWKV_HEAD_DIM = 16, 64
RWKV_WIDTH = RWKV_HEADS * RWKV_HEAD_DIM
RWKV_DECAY_RANK, RWKV_ICLR_RANK, RWKV_GATE_RANK = 128, 128, 480
RWKV_SPLITS = (RWKV_WIDTH, RWKV_WIDTH, RWKV_WIDTH, RWKV_DECAY_RANK, RWKV_DECAY_RANK,
               RWKV_ICLR_RANK, RWKV_ICLR_RANK, RWKV_GATE_RANK)
RWKV_IN = sum(RWKV_SPLITS)
RWKV_IN_PAD = 4096
BRANCH_WIDTH = 1024
N_BRANCHES = 3
N_EXPERTS, TOP_K, EXPERT_FF = 32, 4, 512
SWIGLU_ALPHA, SWIGLU_LIMIT = 1.702, 7.0
DISPATCH_BLOCK = 128
DEEPNORM_ALPHA = (2 * DEPTH) ** 0.25
LN_EPS, RMS_EPS, GN_EPS = 1e-5, 1e-6, 64e-5

IN_SPLITS = (1024, 256, 256, MLA_Q_RANK, MLA_KV_RANK, MLA_ROPE_DIM, RWKV_IN, N_BRANCHES * D_MODEL)
SMALL_SPLITS = IN_SPLITS[:6]
SMALL_WIDTH = 3072
RWKV_COL0 = N_BRANCHES * D_MODEL
SMALL_COL0 = RWKV_COL0 + RWKV_IN_PAD

ROW_TILE = 256
LAT_TILES = SEQ // ROW_TILE
WKV_CHUNK = 64
WKV_PAIRS = RWKV_HEADS // 2
WKV_CHUNKS = N_TOK // WKV_CHUNK
WKV_CTX_CHUNKS = CTX_LEN // WKV_CHUNK
WKV_PREP_CHUNKS = 12
ATTN_KV_TILE = 768
TOKEN_SLAB = (8, 512)
EXPERT_SLAB = (8, 512)
VMEM_LIMIT = 48 * 1024 * 1024
MOE_VMEM_LIMIT = 56 * 1024 * 1024


def _params(*sem, vmem=VMEM_LIMIT):
    return pltpu.CompilerParams(dimension_semantics=sem, vmem_limit_bytes=vmem)


def _dot(a, b):
    return jnp.dot(a.astype(BF16), b.astype(BF16), preferred_element_type=F32)


def _dot_t(a, b):
    return lax.dot_general(a.astype(BF16), b.astype(BF16), (((1,), (1,)), ((), ())), preferred_element_type=F32)


def _tdot(a, b):
    return lax.dot_general(a.astype(BF16), b.astype(BF16), (((0,), (0,)), ((), ())), preferred_element_type=F32)


def _mm_kernel(a_ref, b_ref, bias_ref, o_ref):
    o_ref[...] = (_dot(a_ref[...], b_ref[...]) + bias_ref[...]).astype(o_ref.dtype)


def _pick(n, prefs):
    for p in prefs:
        if n % p == 0:
            return p
    return n


def matmul(a, b, bias=None, out_dtype=F32, tm=None, tn=None, layer=None, name="matmul"):
    m, kdim = a.shape
    n = b.shape[-1]
    tm = tm or _pick(m, (768, 512, 256, 128, 16))
    tn = tn or _pick(n, (512, 256, 128))
    if bias is None:
        bias = jnp.zeros((n,), F32)
    bias = bias.reshape(1, n).astype(F32)
    if layer is None:
        b_spec = pl.BlockSpec((kdim, tn), lambda i, j: (0, j))
    else:
        b_spec = pl.BlockSpec((None, kdim, tn), lambda i, j: (layer, 0, j))
    return pl.pallas_call(
        _mm_kernel,
        grid=(m // tm, n // tn),
        in_specs=[pl.BlockSpec((tm, kdim), lambda i, j: (i, 0)),
                  b_spec,
                  pl.BlockSpec((1, tn), lambda i, j: (0, j))],
        out_specs=pl.BlockSpec((tm, tn), lambda i, j: (i, j)),
        out_shape=jax.ShapeDtypeStruct((m, n), out_dtype),
        compiler_params=_params("parallel", "arbitrary"),
        name=name,
    )(a, b, bias)


def _router_kernel(x_ref, w_ref, b_ref, o_ref):
    x = x_ref[...]
    w = w_ref[...]
    xh = x.astype(BF16)
    xl = (x - xh.astype(F32)).astype(BF16)
    wh = w.astype(BF16)
    wl = (w - wh.astype(F32)).astype(BF16)
    acc = jnp.dot(xh, wh, preferred_element_type=F32)
    acc += jnp.dot(xh, wl, preferred_element_type=F32)
    acc += jnp.dot(xl, wh, preferred_element_type=F32)
    o_ref[...] = acc + b_ref[...]


def router_logits(tokens, router_w, router_b):
    n_tok = tokens.shape[0]
    npad = 128
    w = jnp.zeros((D_MODEL, npad), F32).at[:, :N_EXPERTS].set(router_w)
    b = jnp.zeros((1, npad), F32).at[0, :N_EXPERTS].set(router_b)
    out = pl.pallas_call(
        _router_kernel,
        grid=(n_tok // ROW_TILE,),
        in_specs=[pl.BlockSpec((ROW_TILE, D_MODEL), lambda i: (i, 0)),
                  pl.BlockSpec((D_MODEL, npad), lambda i: (0, 0)),
                  pl.BlockSpec((1, npad), lambda i: (0, 0))],
        out_specs=pl.BlockSpec((ROW_TILE, npad), lambda i: (i, 0)),
        out_shape=jax.ShapeDtypeStruct((n_tok, npad), F32),
        compiler_params=_params("parallel"),
        name="router",
    )(tokens, w, b)
    return out[:, :N_EXPERTS]


def _mod_index(i):
    return (i // LAT_TILES, 0, 0)


def _modulate_kernel(x_ref, sc_ref, sh_ref, o_ref):
    o_ref[...] = (x_ref[...] * (1.0 + sc_ref[0]) + sh_ref[0]).astype(o_ref.dtype)


def modulate(x, sc, sh, out_dtype=BF16):
    n = x.shape[0]
    return pl.pallas_call(
        _modulate_kernel,
        grid=(n // ROW_TILE,),
        in_specs=[pl.BlockSpec((ROW_TILE, D_MODEL), lambda i: (i, 0)),
                  pl.BlockSpec((1, 1, D_MODEL), _mod_index),
                  pl.BlockSpec((1, 1, D_MODEL), _mod_index)],
        out_specs=pl.BlockSpec((ROW_TILE, D_MODEL), lambda i: (i, 0)),
        out_shape=jax.ShapeDtypeStruct((n, D_MODEL), out_dtype),
        compiler_params=_params("parallel"),
        name="modulate",
    )(x, sc, sh)


def _layernorm(z, g, b):
    mu = jnp.mean(z, axis=-1, keepdims=True)
    zc = z - mu
    var = jnp.mean(zc * zc, axis=-1, keepdims=True)
    return zc * lax.rsqrt(var + LN_EPS) * g + b


def _postln_mod_kernel(x_ref, m_ref, g_ref, sc_ref, sh_ref, lng_ref, lnb_ref, y_ref, t_ref, tb_ref):
    y = _layernorm(DEEPNORM_ALPHA * x_ref[...] + g_ref[0] * m_ref[...], lng_ref[...], lnb_ref[...])
    y_ref[...] = y
    t = y * (1.0 + sc_ref[0]) + sh_ref[0]
    t_ref[...] = t
    for s in range(TOKEN_SLAB[0]):
        tb_ref[:, s, :] = t[:, s * TOKEN_SLAB[1]:(s + 1) * TOKEN_SLAB[1]]


def post_ln(x, m, g, ln_g, ln_b, sc, sh):
    n = x.shape[0]
    row = pl.BlockSpec((ROW_TILE, D_MODEL), lambda i: (i, 0))
    mod = pl.BlockSpec((1, 1, D_MODEL), _mod_index)
    vec = pl.BlockSpec((1, D_MODEL), lambda i: (0, 0))
    f32_rows = jax.ShapeDtypeStruct((n, D_MODEL), F32)
    return pl.pallas_call(
        _postln_mod_kernel,
        grid=(n // ROW_TILE,),
        in_specs=[row, row, mod, mod, mod, vec, vec],
        out_specs=[row, row, pl.BlockSpec((ROW_TILE,) + TOKEN_SLAB, lambda i: (i, 0, 0))],
        out_shape=[f32_rows, f32_rows, jax.ShapeDtypeStruct((n,) + TOKEN_SLAB, F32)],
        compiler_params=_params("parallel"),
        name="post_ln_mod",
    )(x, m, g, sc, sh, ln_g.reshape(1, D_MODEL), ln_b.reshape(1, D_MODEL))


def _attn_kernel(q_ref, k_ref, v_ref, o_ref, *, nkv, tk, unroll):
    g, tq, d = q_ref.shape[1:]
    dv = v_ref.shape[-1]
    rows = g * tq
    q = q_ref[0].reshape(rows, d)

    def body(j, carry):
        m_prev, l_prev, acc = carry
        start = pl.multiple_of(j * tk, tk)
        kj = k_ref[0, pl.ds(start, tk), :]
        vj = v_ref[0, pl.ds(start, tk), :]
        s = lax.dot_general(q, kj, (((1,), (1,)), ((), ())), preferred_element_type=F32)
        m_new = jnp.maximum(m_prev, jnp.max(s, axis=-1, keepdims=True))
        alpha = jnp.exp(m_prev - m_new)
        p = jnp.exp(s - m_new)
        l_new = alpha * l_prev + jnp.sum(p, axis=-1, keepdims=True)
        acc = alpha * acc + jnp.dot(p.astype(BF16), vj, preferred_element_type=F32)
        return m_new, l_new, acc

    init = (jnp.full((rows, 1), -jnp.inf, F32), jnp.zeros((rows, 1), F32), jnp.zeros((rows, dv), F32))
    _, l, acc = lax.fori_loop(0, nkv, body, init, unroll=unroll)
    o = acc / l
    for gi in range(g):
        o_ref[:, gi * dv:(gi + 1) * dv] = o[gi * tq:(gi + 1) * tq].astype(o_ref.dtype)


def attention(q, k, v, q_rows, kv_rows, tq, tk, unroll=2, name="attn"):
    hkv, g, _, d = q.shape
    dv = v.shape[-1]
    q0, nq = q_rows
    k0, nk = kv_rows
    assert q0 % tq == 0 and nq % tq == 0 and k0 % nk == 0 and nk % tk == 0
    qb, kb = q0 // tq, k0 // nk
    return pl.pallas_call(
        functools.partial(_attn_kernel, nkv=nk // tk, tk=tk, unroll=unroll),
        grid=(hkv, nq // tq),
        in_specs=[pl.BlockSpec((1, g, tq, d), lambda h, i: (h, 0, i + qb, 0)),
                  pl.BlockSpec((1, nk, d), lambda h, i: (h, kb, 0)),
                  pl.BlockSpec((1, nk, dv), lambda h, i: (h, kb, 0))],
        out_specs=pl.BlockSpec((tq, g * dv), lambda h, i: (i, h)),
        out_shape=jax.ShapeDtypeStruct((nq, hkv * g * dv), BF16),
        compiler_params=_params("parallel", "arbitrary"),
        name=name,
    )(q, k, v)


def lat_ctx_attention(q, k, v, tq, name):
    o_l = attention(q, k, v, (0, SEQ), (0, N_TOK), tq, ATTN_KV_TILE, unroll=N_TOK // ATTN_KV_TILE,
                    name=name + "_lat")
    o_c = attention(q, k, v, (SEQ, CTX_LEN), (SEQ, CTX_LEN), min(tq, CTX_LEN), CTX_LEN, unroll=1,
                    name=name + "_ctx")
    return jnp.concatenate([o_l, o_c], axis=0)


def _bd(x, m_left, m_right):
    return jnp.concatenate([x * m_left, x * m_right], axis=0)


def _cumsum_f32(tri, x):
    x1 = x.astype(BF16)
    r1 = x - x1.astype(F32)
    x2 = r1.astype(BF16)
    x3 = (r1 - x2.astype(F32)).astype(BF16)
    t = tri.astype(BF16)
    return (jnp.dot(t, x1, preferred_element_type=F32) + jnp.dot(t, x2, preferred_element_type=F32)
            + jnp.dot(t, x3, preferred_element_type=F32))


def _wkv_prep_kernel(r_ref, lw_ref, k_ref, v_ref, a_ref, b_ref,
                     wr_ref, u_ref, y0_ref, mrb_ref, bh_ref, g_ref, pc_ref, *, reverse):
    c = WKV_CHUNK
    n = RWKV_HEAD_DIM
    row = lax.broadcasted_iota(jnp.int32, (c, c), 0)
    col = lax.broadcasted_iota(jnp.int32, (c, c), 1)
    tri = ((col >= row) if reverse else (col <= row)).astype(F32)
    row2 = lax.broadcasted_iota(jnp.int32, (2 * c, 2 * c), 0) % c
    col2 = lax.broadcasted_iota(jnp.int32, (2 * c, 2 * c), 1) % c
    if reverse:
        incl, strict = col2 >= row2, col2 > row2
    else:
        incl, strict = col2 <= row2, col2 < row2
    lane = lax.broadcasted_iota(jnp.int32, (1, 2 * n), 1)
    m_left = (lane < n).astype(F32)
    m_right = 1.0 - m_left
    bd = functools.partial(_bd, m_left=m_left, m_right=m_right)
    last = 0 if reverse else c - 1

    chunks = range(WKV_PREP_CHUNKS)
    rows = [slice(ci * c, (ci + 1) * c) for ci in chunks]
    lw = [lw_ref[rw, :] for rw in rows]
    cum = [_cumsum_f32(tri, z) for z in lw]
    p_incl = [jnp.exp(z) for z in cum]
    p_inv = [jnp.exp(-z) for z in cum]
    pe = [z[last:last + 1, :] for z in p_incl]
    at = [bd(a_ref[rows[ci], :] * jnp.exp(cum[ci] - lw[ci])) for ci in chunks]
    rt = [bd(r_ref[rows[ci], :] * p_incl[ci]) for ci in chunks]
    bt = [bd(b_ref[rows[ci], :] * p_inv[ci]) for ci in chunks]
    kt = [bd(k_ref[rows[ci], :] * p_inv[ci]) for ci in chunks]
    v = [bd(v_ref[rows[ci], :]) for ci in chunks]
    big = [_dot_t(jnp.concatenate([at[ci], rt[ci]], axis=0), jnp.concatenate([bt[ci], kt[ci]], axis=0))
           for ci in chunks]
    l_pow = [jnp.where(strict, z[:2 * c, :2 * c], 0.0) for z in big]
    lm = [jnp.concatenate([jnp.where(strict, z[:2 * c, 2 * c:], 0.0),
                           jnp.where(incl, z[2 * c:, 2 * c:], 0.0)], axis=0) for z in big]
    fold = lambda z: z[:c] + z[c:]
    for ci in chunks:
        wr_ref[ci, 0, c:, :] = fold(rt[ci]).astype(BF16)
        mrb_ref[ci, 0] = fold(jnp.where(incl, big[ci][2 * c:, :2 * c], 0.0)).astype(BF16)
        bh_ref[ci, 0] = fold(bt[ci] * pe[ci]).astype(BF16)
        pc_ref[ci, 0] = jnp.broadcast_to(pe[ci], (8, 2 * n))
    lv = [_dot(lm[ci], v[ci]) for ci in chunks]
    for ci in chunks:
        y0_ref[ci, 0] = fold(lv[ci][2 * c:])
        g_ref[ci, 0] = fold(_tdot(v[ci], kt[ci] * pe[ci]))
    x = [jnp.concatenate([at[ci], lv[ci][:2 * c]], axis=1) for ci in chunks]
    x = [x[ci] + _dot(l_pow[ci], x[ci]) for ci in chunks]
    span = 1
    while span * 2 < c:
        l_pow = [_dot(z, z) for z in l_pow]
        x = [x[ci] + _dot(l_pow[ci], x[ci]) for ci in chunks]
        span *= 2
    for ci in chunks:
        wr_ref[ci, 0, :c, :] = fold(x[ci][:, :2 * n]).astype(BF16)
        u_ref[ci, 0] = fold(x[ci][:, 2 * n:])


def _wkv_chunk_fwd(s):
    return (s + SEQ // WKV_CHUNK) % WKV_CHUNKS


def _wkv_chunk_bwd(s):
    return WKV_CHUNKS - 1 - s


def wkv_prep(r, lw, k, v, a, b, reverse):
    c, n2 = WKV_CHUNK, 2 * RWKV_HEAD_DIM
    cb = WKV_PREP_CHUNKS
    spec = pl.BlockSpec((cb * c, n2), lambda p, i: (i, p))

    def out(rows, dtype):
        return (pl.BlockSpec((cb, 1, rows, n2), lambda p, i: (i, p, 0, 0)),
                jax.ShapeDtypeStruct((WKV_CHUNKS, WKV_PAIRS, rows, n2), dtype))

    outs = [out(2 * c, BF16), out(c, F32), out(c, F32), out(c, BF16), out(c, BF16), out(c, F32), out(8, F32)]
    return pl.pallas_call(
        functools.partial(_wkv_prep_kernel, reverse=reverse),
        grid=(WKV_PAIRS, WKV_CHUNKS // cb),
        in_specs=[spec] * 6,
        out_specs=[o[0] for o in outs],
        out_shape=[o[1] for o in outs],
        compiler_params=_params("parallel", "parallel"),
        name="wkv_prep_bwd" if reverse else "wkv_prep_fwd",
    )(r, lw, k, v, a, b)


def _wkv_scan_kernel(*refs):
    c, n = WKV_CHUNK, RWKV_HEAD_DIM
    ins, (yf_ref, yb_ref, s_ref) = refs[:14], refs[14:]

    @pl.when(pl.program_id(0) == 0)
    def _():
        s_ref[...] = jnp.zeros_like(s_ref)

    jobs = [(d, p) for d in range(2) for p in range(WKV_PAIRS)]
    ref = lambda d, k: ins[7 * d + k]
    lane = lax.broadcasted_iota(jnp.int32, (1, 2 * n), 1)
    m_left = (lane < n).astype(F32)
    m_right = 1.0 - m_left
    bd = functools.partial(_bd, m_left=m_left, m_right=m_right)
    row2 = lax.broadcasted_iota(jnp.int32, (2 * n, 2 * n), 0) // n
    col2 = lax.broadcasted_iota(jnp.int32, (2 * n, 2 * n), 1) // n
    same_head = row2 == col2
    s0 = [s_ref[d, p] for d, p in jobs]
    z = [_dot_t(ref(d, 0)[0, p], s0[j]) for j, (d, p) in enumerate(jobs)]
    sa = [z[j][:c] + ref(d, 1)[0, p] for j, (d, p) in enumerate(jobs)]
    for j, (d, p) in enumerate(jobs):
        s_ref[d, p] = (s0[j] * ref(d, 6)[0, p, 0:1, :] + bd(ref(d, 5)[0, p])
                       + jnp.where(same_head, _tdot(sa[j], ref(d, 4)[0, p]), 0.0))
    for j, (d, p) in enumerate(jobs):
        y = z[j][c:] + ref(d, 2)[0, p] + _dot(ref(d, 3)[0, p], bd(sa[j]))
        (yf_ref, yb_ref)[d][:, p * 2 * n:(p + 1) * 2 * n] = y


def wkv_scan(prep_f, prep_b):
    c, n2 = WKV_CHUNK, 2 * RWKV_HEAD_DIM

    def specs(chunk_of):
        return [pl.BlockSpec((1, WKV_PAIRS) + a.shape[2:], lambda s: (chunk_of(s), 0, 0, 0)) for a in prep_f]

    y_shape = jax.ShapeDtypeStruct((N_TOK, RWKV_WIDTH), F32)
    return pl.pallas_call(
        _wkv_scan_kernel,
        grid=(WKV_CHUNKS,),
        in_specs=specs(_wkv_chunk_fwd) + specs(_wkv_chunk_bwd),
        out_specs=[pl.BlockSpec((c, RWKV_WIDTH), lambda s: (_wkv_chunk_fwd(s), 0)),
                   pl.BlockSpec((c, RWKV_WIDTH), lambda s: (_wkv_chunk_bwd(s), 0))],
        out_shape=[y_shape, y_shape],
        scratch_shapes=[pltpu.VMEM((2, WKV_PAIRS, n2, n2), F32)],
        compiler_params=_params("arbitrary"),
        name="wkv_scan",
    )(*prep_f, *prep_b)


MOE_CAST_ROWS = 256


def _moe_kernel(be_ref, nx_ref, idx_ref, idx_next_ref, tok_hbm, w1_hbm, b1_ref, w2_hbm, b2_ref, g_ref, o_ref,
                w1f_ref, w2f_ref, w1b_ref, w2b_ref, xbuf, sem, wsem, *, layer):
    i = pl.program_id(0)
    cur = i % 2

    def weight_copies(e):
        return (pltpu.make_async_copy(w1_hbm.at[layer, e], w1f_ref, wsem.at[0]),
                pltpu.make_async_copy(w2_hbm.at[layer, e], w2f_ref, wsem.at[1]))

    @pl.when(i == 0)
    def _():
        for cp in weight_copies(be_ref[0]):
            cp.start()

    def row_copy(idx, r, buf):
        return pltpu.make_async_copy(tok_hbm.at[idx[0, 0, r]], xbuf.at[buf, r], sem.at[buf])

    def start_rows(idx, buf):
        def body(r, carry):
            row_copy(idx, r, buf).start()
            return carry
        lax.fori_loop(0, DISPATCH_BLOCK, body, 0, unroll=8)

    @pl.when(i == 0)
    def _():
        start_rows(idx_ref, 0)

    @pl.when(i + 1 < pl.num_programs(0))
    def _():
        start_rows(idx_next_ref, 1 - cur)

    new_expert = (i == 0) | (be_ref[i] != be_ref[jnp.maximum(i - 1, 0)])

    @pl.when(new_expert)
    def _():
        for cp in weight_copies(be_ref[i]):
            cp.wait()

        def cast1(c, carry):
            rows = pl.ds(pl.multiple_of(c * MOE_CAST_ROWS, MOE_CAST_ROWS), MOE_CAST_ROWS)
            w1b_ref[rows, :] = w1f_ref[rows, :].astype(BF16)
            return carry

        def cast2(c, carry):
            rows = pl.ds(pl.multiple_of(c * MOE_CAST_ROWS, MOE_CAST_ROWS), MOE_CAST_ROWS)
            w2b_ref[rows, :] = w2f_ref[rows, :].astype(BF16)
            return carry

        lax.fori_loop(0, D_MODEL // MOE_CAST_ROWS, cast1, 0)
        lax.fori_loop(0, EXPERT_FF // MOE_CAST_ROWS, cast2, 0)

        @pl.when(nx_ref[i] >= 0)
        def _():
            for cp in weight_copies(nx_ref[i]):
                cp.start()

    def wait_row(r, carry):
        row_copy(idx_ref, r, cur).wait()
        return carry
    lax.fori_loop(0, DISPATCH_BLOCK, wait_row, 0, unroll=8)

    x = xbuf[cur].reshape(DISPATCH_BLOCK, D_MODEL).astype(BF16)
    u = jnp.dot(x, w1b_ref[...], preferred_element_type=F32) + b1_ref[0]
    u_glu = jnp.minimum(u[:, :EXPERT_FF], SWIGLU_LIMIT)
    u_lin = jnp.clip(u[:, EXPERT_FF:], -SWIGLU_LIMIT, SWIGLU_LIMIT)
    act = (u_glu * jax.nn.sigmoid(SWIGLU_ALPHA * u_glu) * (u_lin + 1.0)).astype(BF16)
    gate = g_ref[...]
    ew = EXPERT_SLAB[1]
    for s in range(EXPERT_SLAB[0]):
        cols = slice(s * ew, (s + 1) * ew)
        y = jnp.dot(act, w2b_ref[:, cols], preferred_element_type=F32) + b2_ref[0, :, cols]
        o_ref[:, s, :] = y * gate


def moe_experts(token_slabs, slot_token, slot_gate, block_expert, w1, b1, w2, b2, layer):
    n_slots = slot_token.shape[0]
    n_blocks = n_slots // DISPATCH_BLOCK
    idx = slot_token.reshape(n_blocks, 1, DISPATCH_BLOCK)
    idx_spec = lambda nxt: pl.BlockSpec((1, 1, DISPATCH_BLOCK),
                                        lambda i, be, nx: (jnp.minimum(i + nxt, n_blocks - 1), 0, 0),
                                        memory_space=pltpu.SMEM)
    later = jnp.where(block_expert[None, :] > block_expert[:, None], block_expert[None, :], N_EXPERTS)
    next_expert = jnp.min(later, axis=1)
    next_expert = jnp.where(next_expert == N_EXPERTS, -1, next_expert).astype(jnp.int32)
    grid_spec = pltpu.PrefetchScalarGridSpec(
        num_scalar_prefetch=2,
        grid=(n_blocks,),
        in_specs=[idx_spec(0), idx_spec(1),
                  pl.BlockSpec(memory_space=pl.ANY),
                  pl.BlockSpec(memory_space=pl.ANY),
                  pl.BlockSpec((1, 1, 2 * EXPERT_FF), lambda i, be, nx: (be[i], 0, 0)),
                  pl.BlockSpec(memory_space=pl.ANY),
                  pl.BlockSpec((1, 1, D_MODEL), lambda i, be, nx: (be[i], 0, 0)),
                  pl.BlockSpec((DISPATCH_BLOCK, 1), lambda i, be, nx: (i, 0))],
        out_specs=pl.BlockSpec((DISPATCH_BLOCK,) + EXPERT_SLAB, lambda i, be, nx: (i, 0, 0)),
        scratch_shapes=[pltpu.VMEM((D_MODEL, 2 * EXPERT_FF), F32), pltpu.VMEM((EXPERT_FF, D_MODEL), F32),
                        pltpu.VMEM((D_MODEL, 2 * EXPERT_FF), BF16), pltpu.VMEM((EXPERT_FF, D_MODEL), BF16),
                        pltpu.VMEM((2, DISPATCH_BLOCK) + TOKEN_SLAB, F32), pltpu.SemaphoreType.DMA((2,)),
                        pltpu.SemaphoreType.DMA((2,))],
    )
    return pl.pallas_call(
        functools.partial(_moe_kernel, layer=layer),
        grid_spec=grid_spec,
        out_shape=jax.ShapeDtypeStruct((n_slots,) + EXPERT_SLAB, F32),
        compiler_params=_params("arbitrary", vmem=MOE_VMEM_LIMIT),
        name="moe_experts",
    )(block_expert, next_expert, idx, idx, token_slabs, w1, b1.reshape(N_EXPERTS, 1, -1), w2,
      b2.reshape(N_EXPERTS, 1, -1), slot_gate.reshape(n_slots, 1))


COMBINE_ROWS = 64


def _combine_ln_kernel(idx_ref, idx_next_ref, y_hbm, x_ref, g_ref, lng_ref, lnb_ref, o_ref, buf, sem):
    i = pl.program_id(0)
    cur = i % 2

    def row_copy(idx, r, j, b):
        return pltpu.make_async_copy(y_hbm.at[idx[0, 0, r * TOP_K + j]], buf.at[b, j, r], sem.at[b])

    def start_rows(idx, b):
        def body(r, carry):
            for j in range(TOP_K):
                row_copy(idx, r, j, b).start()
            return carry
        lax.fori_loop(0, COMBINE_ROWS, body, 0, unroll=4)

    @pl.when(i == 0)
    def _():
        start_rows(idx_ref, 0)

    @pl.when(i + 1 < pl.num_programs(0))
    def _():
        start_rows(idx_next_ref, 1 - cur)

    def wait_rows(r, carry):
        for j in range(TOP_K):
            row_copy(idx_ref, r, j, cur).wait()
        return carry
    lax.fori_loop(0, COMBINE_ROWS, wait_rows, 0, unroll=4)

    f = buf[cur, 0]
    for j in range(1, TOP_K):
        f = f + buf[cur, j]
    f = f.reshape(COMBINE_ROWS, D_MODEL)
    o_ref[...] = _layernorm(DEEPNORM_ALPHA * x_ref[...] + g_ref[0] * f, lng_ref[...], lnb_ref[...])


def moe_combine_ln(y, slot, x, g, ln_g, ln_b):
    n = x.shape[0]
    nb = n // COMBINE_ROWS
    idx = slot.reshape(nb, 1, COMBINE_ROWS * TOP_K)
    idx_spec = lambda nxt: pl.BlockSpec((1, 1, COMBINE_ROWS * TOP_K),
                                        lambda i: (jnp.minimum(i + nxt, nb - 1), 0, 0),
                                        memory_space=pltpu.SMEM)
    row = pl.BlockSpec((COMBINE_ROWS, D_MODEL), lambda i: (i, 0))
    mod = pl.BlockSpec((1, 1, D_MODEL), lambda i: (i // (SEQ // COMBINE_ROWS), 0, 0))
    vec = pl.BlockSpec((1, D_MODEL), lambda i: (0, 0))
    return pl.pallas_call(
        _combine_ln_kernel,
        grid=(nb,),
        in_specs=[idx_spec(0), idx_spec(1), pl.BlockSpec(memory_space=pl.ANY), row, mod, vec, vec],
        out_specs=row,
        out_shape=jax.ShapeDtypeStruct((n, D_MODEL), F32),
        scratch_shapes=[pltpu.VMEM((2, TOP_K, COMBINE_ROWS) + EXPERT_SLAB, F32), pltpu.SemaphoreType.DMA((2,))],
        compiler_params=_params("arbitrary"),
        name="moe_combine_ln",
    )(idx, idx, y, x, g, ln_g.reshape(1, D_MODEL), ln_b.reshape(1, D_MODEL))


def moe(tokens, token_slabs, router_w, router_b, w1, b1, w2, b2, layer):
    n_tok = tokens.shape[0]
    n_assign = n_tok * TOP_K
    n_blocks = -(-(n_assign + N_EXPERTS * (DISPATCH_BLOCK - 1)) // DISPATCH_BLOCK)
    n_slots = n_blocks * DISPATCH_BLOCK
    logits = router_logits(tokens, router_w, router_b)
    top_val, top_idx = lax.top_k(logits, TOP_K)
    gate = jax.nn.softmax(top_val, axis=-1).reshape(-1)
    expert = top_idx.reshape(-1)
    onehot = (expert[:, None] == jnp.arange(N_EXPERTS, dtype=expert.dtype)[None, :]).astype(jnp.int32)
    before = jnp.cumsum(onehot, axis=0) - onehot
    counts = before[-1] + onehot[-1]
    padded = (counts + DISPATCH_BLOCK - 1) // DISPATCH_BLOCK * DISPATCH_BLOCK
    pad_end = jnp.cumsum(padded)
    pad_start = pad_end - padded
    slot = jnp.sum(onehot * (before + pad_start[None, :]), axis=1).astype(jnp.int32)
    slot_assign = jnp.full((n_slots,), -1, jnp.int32).at[slot].set(jnp.arange(n_assign, dtype=jnp.int32))
    filled = slot_assign >= 0
    src = jnp.maximum(slot_assign, 0)
    slot_token = jnp.where(filled, src // TOP_K, 0)
    slot_gate = jnp.where(filled, gate[src], 0.0)
    block_start = jnp.arange(n_blocks, dtype=pad_end.dtype) * DISPATCH_BLOCK
    block_expert = jnp.minimum(jnp.sum(pad_end[None, :] <= block_start[:, None], axis=1),
                               N_EXPERTS - 1).astype(jnp.int32)
    y = moe_experts(token_slabs, slot_token, slot_gate, block_expert, w1, b1, w2, b2, layer)
    return y, slot.reshape(n_tok, TOP_K)


def _rope_lanes(x, cos, sin, half):
    lane = lax.broadcasted_iota(jnp.int32, x.shape, 1)
    up = pltpu.roll(x, 128 - half, 1)
    down = pltpu.roll(x, half, 1)
    return x * cos + jnp.where(lane % (2 * half) < half, up, down) * sin


def _rms(x, g):
    return x * lax.rsqrt(jnp.mean(x * x, axis=-1, keepdims=True) + RMS_EPS) * g


def _attn_prep_kernel(pa_ref, pb_ref, pc_ref, gq_ref, gk_ref, gmq_ref, gmkv_ref, cosa_ref, sina_ref,
                      cosb_ref, sinb_ref, wuq_ref, wukv_ref, qg_ref, kg_ref, vg_ref, qc_ref, kc_ref, vm_ref):
    hd = GQA_HEAD_DIM
    ca, sa, cb, sb = cosa_ref[...], sina_ref[...], cosb_ref[...], sinb_ref[...]
    grp = GQA_HEADS // GQA_KV_HEADS
    for h in range(GQA_HEADS):
        q = _rope_lanes(_rms(pa_ref[:, h * hd:(h + 1) * hd], gq_ref[...]), ca, sa, hd // 4)
        qg_ref[h // grp, h % grp] = (q * GQA_HEAD_DIM ** -0.5).astype(BF16)
    for h in range(GQA_KV_HEADS):
        k = _rope_lanes(_rms(pb_ref[:, h * hd:(h + 1) * hd], gk_ref[...]), ca, sa, hd // 4)
        kg_ref[h] = k.astype(BF16)
        vg_ref[h] = pb_ref[:, (GQA_KV_HEADS + h) * hd:(GQA_KV_HEADS + h + 1) * hd].astype(BF16)
    dkvn = _rms(pb_ref[:, 2 * GQA_KV_HEADS * hd:], gmkv_ref[...])
    dqn = _rms(pc_ref[:, :MLA_Q_RANK], gmq_ref[...])
    k_rope = _rope_lanes(pc_ref[:, MLA_Q_RANK:], cb, sb, MLA_ROPE_DIM // 4)
    qb = _dot(dqn, wuq_ref[...])
    kvb = _dot(dkvn, wukv_ref[...])
    scale = MLA_QK_DIM ** -0.5
    for h in range(MLA_HEADS):
        base = h * MLA_CAT_DIM
        q_nope = qb[:, base:base + 128]
        q_rope = _rope_lanes(qb[:, base + 128:base + 256], cb, sb, MLA_ROPE_DIM // 4)
        qc_ref[h, 0] = (jnp.concatenate([q_nope, q_rope], axis=1) * scale).astype(BF16)
        kc_ref[h] = jnp.concatenate([kvb[:, base:base + 128], k_rope], axis=1).astype(BF16)
        vm_ref[h] = kvb[:, base + 128:base + 256].astype(BF16)


def _rope_lane_tables(rot_dim):
    n_freq = rot_dim // 4
    row = jnp.repeat(jnp.arange(SEQ // GRID_W, dtype=F32), GRID_W)
    col = (jnp.arange(SEQ) % GRID_W).astype(F32)
    freqs = ROPE_THETA ** (-jnp.arange(n_freq, dtype=F32) / n_freq)
    ar, ac = row[:, None] * freqs, col[:, None] * freqs
    cos = jnp.concatenate([jnp.cos(ar), jnp.cos(ar), jnp.cos(ac), jnp.cos(ac)], axis=1)
    sin = jnp.concatenate([-jnp.sin(ar), jnp.sin(ar), -jnp.sin(ac), jnp.sin(ac)], axis=1)
    cos = jnp.pad(cos, ((0, CTX_LEN), (0, 128 - rot_dim)), constant_values=1.0)
    sin = jnp.pad(sin, ((0, CTX_LEN), (0, 128 - rot_dim)))
    return cos, sin


def attn_prep(proj, gqa_q_norm, gqa_k_norm, mla_q_norm, mla_kv_norm, mla_w_uq, mla_w_ukv):
    t = N_TOK
    cos_a, sin_a = _rope_lane_tables(GQA_HEAD_DIM)
    cos_b, sin_b = _rope_lane_tables(MLA_ROPE_DIM)
    w_uq = jnp.pad(mla_w_uq.reshape(MLA_Q_RANK, MLA_HEADS, MLA_QK_DIM),
                   ((0, 0), (0, 0), (0, MLA_CAT_DIM - MLA_QK_DIM))).reshape(MLA_Q_RANK, -1).astype(BF16)
    w_ukv = mla_w_ukv.astype(BF16)
    first = SMALL_COL0 // 1024
    col = lambda j: pl.BlockSpec((ROW_TILE, 1024), lambda i: (i, first + j))
    vec = lambda n: pl.BlockSpec((1, n), lambda i: (0, 0))
    tab = pl.BlockSpec((ROW_TILE, 128), lambda i: (i, 0))
    full = lambda a: pl.BlockSpec(a.shape, lambda i: (0, 0))
    grp = GQA_HEADS // GQA_KV_HEADS
    hd = GQA_HEAD_DIM
    outs = [
        (pl.BlockSpec((GQA_KV_HEADS, grp, ROW_TILE, hd), lambda i: (0, 0, i, 0)), (GQA_KV_HEADS, grp, t, hd)),
        (pl.BlockSpec((GQA_KV_HEADS, ROW_TILE, hd), lambda i: (0, i, 0)), (GQA_KV_HEADS, t, hd)),
        (pl.BlockSpec((GQA_KV_HEADS, ROW_TILE, hd), lambda i: (0, i, 0)), (GQA_KV_HEADS, t, hd)),
        (pl.BlockSpec((MLA_HEADS, 1, ROW_TILE, MLA_CAT_DIM), lambda i: (0, 0, i, 0)), (MLA_HEADS, 1, t, MLA_CAT_DIM)),
        (pl.BlockSpec((MLA_HEADS, ROW_TILE, MLA_CAT_DIM), lambda i: (0, i, 0)), (MLA_HEADS, t, MLA_CAT_DIM)),
        (pl.BlockSpec((MLA_HEADS, ROW_TILE, MLA_V_DIM), lambda i: (0, i, 0)), (MLA_HEADS, t, MLA_V_DIM)),
    ]
    return pl.pallas_call(
        _attn_prep_kernel,
        grid=(t // ROW_TILE,),
        in_specs=[col(0), col(1), col(2), vec(hd), vec(hd), vec(MLA_Q_RANK), vec(MLA_KV_RANK),
                  tab, tab, tab, tab, full(w_uq), full(w_ukv)],
        out_specs=[o[0] for o in outs],
        out_shape=[jax.ShapeDtypeStruct(o[1], BF16) for o in outs],
        compiler_params=_params("parallel"),
        name="attn_prep",
    )(proj, proj, proj, gqa_q_norm.reshape(1, -1), gqa_k_norm.reshape(1, -1), mla_q_norm.reshape(1, -1),
      mla_kv_norm.reshape(1, -1), cos_a, sin_a, cos_b, sin_b, w_uq, w_ukv)


def _group_sum(x, ones_bd):
    hi = x.astype(BF16)
    lo = (x - hi.astype(F32)).astype(BF16)
    return jnp.dot(hi, ones_bd, preferred_element_type=F32) + jnp.dot(lo, ones_bd, preferred_element_type=F32)


def _softplus(z):
    return jnp.maximum(z, 0.0) + jnp.log(1.0 + jnp.exp(-jnp.abs(z)))


def _rwkv_prep_kernel(p_ref, prev_ref, next_ref, mu_ref, wup_ref, w0_ref, aup_ref, a0_ref, gup_ref,
                      kk_ref, ka_ref, rk_ref, ones_ref,
                      r_ref, v_ref, a_ref, lwf_ref, kdf_ref, bf_ref, lwb_ref, kdb_ref, bb_ref, g_ref, bonus_ref):
    i = pl.program_id(0)
    w = RWKV_WIDTH
    p = p_ref[...]
    n = p.shape[0]
    rowid = lax.broadcasted_iota(jnp.int32, (n, 1), 0)
    has_prev = jnp.where(i % LAT_TILES == 0, 0.0, 1.0)
    has_next = jnp.where((i == LAT_TILES - 1) | (i == LAT_TILES), 0.0, 1.0)
    prev = jnp.where(rowid == 0, prev_ref[7:8, :] * has_prev, pltpu.roll(p, 1, 0))
    nxt = jnp.where(rowid == n - 1, next_ref[0:1, :] * has_next, pltpu.roll(p, n - 1, 0))
    p = p + mu_ref[0:1, :] * (prev - p) + mu_ref[1:2, :] * (nxt - p)

    ones_bd = ones_ref[...]
    r, kx, vx = p[:, :w], p[:, w:2 * w], p[:, 2 * w:3 * w]
    low = p[:, 3 * w:]
    g_ref[...] = _dot(jax.nn.sigmoid(low[:, 512:]), gup_ref[...])
    kk = kx * kk_ref[...]
    kk = kk / jnp.maximum(jnp.sqrt(_group_sum(kk * kk, ones_bd)), 1e-12)
    r_ref[...] = r
    v_ref[...] = vx
    a_ref[...] = -kk
    k_sum = jnp.zeros_like(kx)
    for d, (lw_ref, kd_ref, b_ref) in enumerate(((lwf_ref, kdf_ref, bf_ref), (lwb_ref, kdb_ref, bb_ref))):
        wd = low[:, d * 128:(d + 1) * 128]
        ad = low[:, 256 + d * 128:256 + (d + 1) * 128]
        w_lin = _dot(jnp.tanh(wd), wup_ref[d]) + w0_ref[d:d + 1, :]
        lw_ref[...] = -jnp.exp(-_softplus(-w_lin) - 0.5)
        a_gate = jax.nn.sigmoid(_dot(ad, aup_ref[d]) + a0_ref[d:d + 1, :])
        k_d = kx * (1.0 + (a_gate - 1.0) * ka_ref[...])
        kd_ref[...] = k_d
        b_ref[...] = kk * a_gate
        k_sum = k_sum + k_d
    bonus_ref[...] = _group_sum(r * k_sum * rk_ref[...], ones_bd) * vx


def _head_ones():
    h = jnp.arange(RWKV_WIDTH) // RWKV_HEAD_DIM
    return (h[:, None] == h[None, :]).astype(BF16)


def rwkv_prep(proj, rwkv_mu, rwkv_w0, rwkv_w_up, rwkv_a0, rwkv_a_up, rwkv_g_up, rwkv_k_k, rwkv_k_a, rwkv_r_k):
    t, w = N_TOK, RWKV_WIDTH
    cb = RWKV_COL0 // RWKV_IN_PAD
    n8 = t // 8
    per8 = ROW_TILE // 8
    mu = jnp.pad(rwkv_mu, ((0, 6), (0, RWKV_IN_PAD - RWKV_IN)))
    g_up = jnp.pad(rwkv_g_up, ((0, 512 - RWKV_GATE_RANK), (0, 0))).astype(BF16)
    full = lambda a: pl.BlockSpec(a.shape, lambda i: (0,) * a.ndim)
    consts = [mu, rwkv_w_up.astype(BF16), rwkv_w0, rwkv_a_up.astype(BF16), rwkv_a0, g_up,
              rwkv_k_k.reshape(1, w), rwkv_k_a.reshape(1, w), rwkv_r_k.reshape(1, w), _head_ones()]
    out_spec = pl.BlockSpec((ROW_TILE, w), lambda i: (i, 0))
    return pl.pallas_call(
        _rwkv_prep_kernel,
        grid=(t // ROW_TILE,),
        in_specs=[pl.BlockSpec((ROW_TILE, RWKV_IN_PAD), lambda i: (i, cb)),
                  pl.BlockSpec((8, RWKV_IN_PAD), lambda i: (jnp.maximum(i * per8 - 1, 0), cb)),
                  pl.BlockSpec((8, RWKV_IN_PAD), lambda i: (jnp.minimum((i + 1) * per8, n8 - 1), cb))]
                 + [full(a) for a in consts],
        out_specs=[out_spec] * 11,
        out_shape=[jax.ShapeDtypeStruct((t, w), F32)] * 11,
        compiler_params=_params("parallel"),
        name="rwkv_prep",
    )(proj, proj, proj, *consts)


def _rwkv_out_kernel(yf_ref, yb_ref, bonus_ref, g_ref, gng_ref, gnb_ref, ones_ref, o_ref):
    ones_bd = ones_ref[...]
    y = yf_ref[...] + yb_ref[...]
    yc = y - _group_sum(y, ones_bd) * (1.0 / RWKV_HEAD_DIM)
    var = _group_sum(yc * yc, ones_bd) * (1.0 / RWKV_HEAD_DIM)
    yn = yc * lax.rsqrt(var + GN_EPS) * gng_ref[...] + gnb_ref[...]
    o_ref[...] = ((yn + bonus_ref[...]) * g_ref[...]).astype(o_ref.dtype)


def rwkv_out(y_f, y_b, bonus, g, gn_g, gn_b):
    t, w = N_TOK, RWKV_WIDTH
    row = pl.BlockSpec((ROW_TILE, w), lambda i: (i, 0))
    vec = pl.BlockSpec((1, w), lambda i: (0, 0))
    ones = _head_ones()
    return pl.pallas_call(
        _rwkv_out_kernel,
        grid=(t // ROW_TILE,),
        in_specs=[row, row, row, row, vec, vec, pl.BlockSpec(ones.shape, lambda i: (0, 0))],
        out_specs=row,
        out_shape=jax.ShapeDtypeStruct((t, w), BF16),
        compiler_params=_params("parallel"),
        name="rwkv_out",
    )(y_f, y_b, bonus, g, gn_g.reshape(1, w), gn_b.reshape(1, w), ones)


def _merge_kernel(oa_ref, ob_ref, oc_ref, wa_ref, wb_ref, wc_ref, ga_ref, gb_ref, gc_ref, m_ref):
    acc = jax.nn.sigmoid(ga_ref[...]) * _dot(oa_ref[...], wa_ref[0])
    acc += jax.nn.sigmoid(gb_ref[...]) * _dot(ob_ref[...], wb_ref[0])
    acc += jax.nn.sigmoid(gc_ref[...]) * _dot(oc_ref[...], wc_ref[0])
    m_ref[...] = acc.astype(m_ref.dtype)


def merge(oa, ob, oc, proj, w_branch_up):
    t = N_TOK
    tm, tn = 768, 512
    nj = D_MODEL // tn
    wb = w_branch_up.astype(BF16)
    o_spec = pl.BlockSpec((tm, BRANCH_WIDTH), lambda i, j: (i, 0))
    w_spec = lambda b: pl.BlockSpec((1, BRANCH_WIDTH, tn), lambda i, j: (b, 0, j))
    g_spec = lambda b: pl.BlockSpec((tm, tn), lambda i, j: (i, b * nj + j))
    return pl.pallas_call(
        _merge_kernel,
        grid=(t // tm, nj),
        in_specs=[o_spec, o_spec, o_spec, w_spec(0), w_spec(1), w_spec(2), g_spec(0), g_spec(1), g_spec(2)],
        out_specs=pl.BlockSpec((tm, tn), lambda i, j: (i, j)),
        out_shape=jax.ShapeDtypeStruct((t, D_MODEL), BF16),
        compiler_params=_params("parallel", "arbitrary"),
        name="merge",
    )(oa, ob, oc, wb, wb, wb, proj, proj, proj)


def _split(x, sizes):
    return jnp.split(x, np.cumsum(sizes)[:-1].tolist(), axis=-1)


def _in_proj_weight(w_in):
    aq, ak, av, dq, dkv, kr, rwkv, gates = _split(w_in, IN_SPLITS)
    small = jnp.concatenate([aq, ak, av, dkv, dq, kr], axis=1)
    pad = lambda z, n: jnp.pad(z, ((0, 0), (0, n - z.shape[1])))
    return jnp.concatenate([gates, pad(rwkv, RWKV_IN_PAD), pad(small, SMALL_WIDTH)], axis=1).astype(BF16)


def _mixer(h, w_in, gqa_q_norm, gqa_k_norm, mla_q_norm, mla_kv_norm, mla_w_uq, mla_w_ukv,
           rwkv_mu, rwkv_w0, rwkv_w_up, rwkv_a0, rwkv_a_up, rwkv_g_up, rwkv_k_k, rwkv_k_a, rwkv_r_k,
           rwkv_gn_g, rwkv_gn_b, w_branch_up, w_o):
    proj = matmul(h, _in_proj_weight(w_in), name="in_proj")

    qg, kg, vg, qc, kc, vm = attn_prep(proj, gqa_q_norm, gqa_k_norm, mla_q_norm, mla_kv_norm, mla_w_uq, mla_w_ukv)
    oa = lat_ctx_attention(qg, kg, vg, 256, "gqa")
    ob = lat_ctx_attention(qc, kc, vm, 1024, "mla")

    r, v, a, lw_f, kd_f, b_f, lw_b, kd_b, b_b, g, bonus = rwkv_prep(
        proj, rwkv_mu, rwkv_w0, rwkv_w_up, rwkv_a0, rwkv_a_up, rwkv_g_up, rwkv_k_k, rwkv_k_a, rwkv_r_k)
    y_f, y_b = wkv_scan(wkv_prep(r, lw_f, kd_f, v, a, b_f, reverse=False),
                        wkv_prep(r, lw_b, kd_b, v, a, b_b, reverse=True))
    oc = rwkv_out(y_f, y_b, bonus, g, rwkv_gn_g, rwkv_gn_b)

    m = merge(oa, ob, oc, proj, w_branch_up)
    return matmul(m, w_o.astype(BF16), name="out_proj")


def kernel(x, c, ctx, c_ctx, w_ada, b_ada, w_in, gqa_q_norm, gqa_k_norm, mla_q_norm, mla_kv_norm, mla_w_uq,
           mla_w_ukv, rwkv_mu, rwkv_w0, rwkv_w_up, rwkv_a0, rwkv_a_up, rwkv_g_up, rwkv_k_k, rwkv_k_a, rwkv_r_k,
           rwkv_gn_g, rwkv_gn_b, w_branch_up, w_o, ln1_g, ln1_b, router_w, router_b, exp_w_in, exp_b_in,
           exp_w_out, exp_b_out, ln2_g, ln2_b):
    xs = jnp.concatenate([x[0], ctx[0]], axis=0)
    cond = jnp.zeros((16, D_MODEL), F32).at[0].set(jax.nn.silu(c[0])).at[1].set(jax.nn.silu(c_ctx))
    for l in range(DEPTH):
        ada = matmul(cond.astype(BF16), w_ada, bias=b_ada[l], tm=16, tn=512, layer=l, name="adaln")
        sh1, sc1, g1, sh2, sc2, g2 = (z[:2].reshape(2, 1, D_MODEL) for z in _split(ada, (D_MODEL,) * 6))
        h = modulate(xs, sc1, sh1)
        m = _mixer(h, w_in[l], gqa_q_norm[l], gqa_k_norm[l], mla_q_norm[l], mla_kv_norm[l], mla_w_uq[l],
                   mla_w_ukv[l], rwkv_mu[l], rwkv_w0[l], rwkv_w_up[l], rwkv_a0[l], rwkv_a_up[l], rwkv_g_up[l],
                   rwkv_k_k[l], rwkv_k_a[l], rwkv_r_k[l], rwkv_gn_g[l], rwkv_gn_b[l], w_branch_up[l], w_o[l])
        xs, tokens, token_slabs = post_ln(xs, m, g1, ln1_g[l], ln1_b[l], sc2, sh2)
        y, slot = moe(tokens, token_slabs, router_w[l], router_b[l], exp_w_in, exp_b_in[l], exp_w_out,
                      exp_b_out[l], l)
        xs = moe_combine_ln(y, slot, xs, g2, ln2_g[l], ln2_b[l])
    return xs[:SEQ].reshape(1, SEQ, D_MODEL)
```
